```python
import jax, jax.numpy as jnp
from jax import lax
import numpy as np

D_MODEL = 1024
BATCH = 8
SEQ = 2048
DEPTH = 1
DEC_BATCH = 128
DEC_SEQ = 8
PAST_LEN = 16384
PAGE_SIZE = 128

CONV_WIDTH = D_MODEL // 2
CONV_KERNEL = 31
HEAD_DIM = 64
RWKV_WIDTH = D_MODEL
RWKV_HEADS = RWKV_WIDTH // HEAD_DIM
DECAY_RANK = 64
ICLR_RANK = 64
GATE_RANK = 128
GLU_COLS = 2 * CONV_WIDTH
RW_START = GLU_COLS
RW_COLS = 3 * RWKV_WIDTH + DECAY_RANK + ICLR_RANK + GATE_RANK
GATE_START = RW_START + RW_COLS
IN_COLS = GATE_START + 2 * D_MODEL
N_GROUPS = 4
EXPERTS_PER_GROUP = 8
N_EXPERTS = N_GROUPS * EXPERTS_PER_GROUP
TOP_K_IN_GROUP = 2
D_EXPERT = 256
MOE_BLOCK = 128
NORM_EPS = 1e-6
LN_EPS = 1e-5
GN_EPS = 64e-5

kernel_name = "hybrid_conformer_rwkv7_hiermoe_step"


def rmsnorm(x, g):
    xf = x.astype(jnp.float32)
    return xf * lax.rsqrt(jnp.mean(xf * xf, axis=-1, keepdims=True) + NORM_EPS) * g.astype(jnp.float32)


def conformer_conv(glu_in, conv_buf, w_dw, b_dw, ln_g, ln_b, w_out, b_out):
    a, gt = jnp.split(glu_in, 2, axis=-1)
    glu = a * jax.nn.sigmoid(gt)
    full = jnp.concatenate([conv_buf.astype(jnp.float32), glu], axis=1)
    y = lax.conv_general_dilated(full, w_dw.astype(jnp.float32)[:, None, :], window_strides=(1,),
                                 padding='VALID', dimension_numbers=('NWC', 'WIO', 'NWC'),
                                 feature_group_count=CONV_WIDTH) + b_dw
    m = jnp.mean(y, axis=-1, keepdims=True)
    var = jnp.mean(jnp.square(y - m), axis=-1, keepdims=True)
    y = (y - m) * lax.rsqrt(var + LN_EPS) * ln_g + ln_b
    y = jax.nn.silu(y)
    return y @ w_out + b_out, full[:, -(CONV_KERNEL - 1):]


def wkv_scan(S0, r, decay, kk, b, v, k):
    def step(S, xs):
        r_t, w_t, kk_t, b_t, v_t, k_t = xs
        sa = jnp.einsum('bhvk,bhk->bhv', S, kk_t)
        S = S * w_t[:, :, None, :] - sa[..., None] * b_t[:, :, None, :] + v_t[..., None] * k_t[:, :, None, :]
        return S, jnp.einsum('bhvk,bhk->bhv', S, r_t)
    xs = tuple(jnp.moveaxis(a, 1, 0) for a in (r, decay, kk, b, v, k))
    S, ys = lax.scan(step, S0, xs)
    return jnp.moveaxis(ys, 0, 1), S


def rwkv7_mix(p_rw, prev_rw, S0, mu, w0, w2, a0, a2, g2, k_k, k_a, r_k, gn_g, gn_b, w_out):
    B, T, _ = p_rw.shape
    p_prev = jnp.concatenate([prev_rw[:, None, :], p_rw[:, :-1]], axis=1)
    xm = p_rw + mu * (p_prev - p_rw)
    o1, o2, o3 = RWKV_WIDTH, 2 * RWKV_WIDTH, 3 * RWKV_WIDTH
    o4, o5 = o3 + DECAY_RANK, o3 + DECAY_RANK + ICLR_RANK
    r, k, v = xm[..., :o1], xm[..., o1:o2], xm[..., o2:o3]
    wd, ad, gd = xm[..., o3:o4], xm[..., o4:o5], xm[..., o5:]
    w = -jax.nn.softplus(-(w0 + jnp.tanh(wd) @ w2)) - 0.5
    decay = jnp.exp(-jnp.exp(w))
    a = jax.nn.sigmoid(a0 + ad @ a2)
    g = jax.nn.sigmoid(gd) @ g2
    hs = lambda t: t.reshape(B, T, RWKV_HEADS, HEAD_DIM)
    kk = hs(k * k_k)
    kk = kk / jnp.maximum(jnp.linalg.norm(kk, axis=-1, keepdims=True), 1e-12)
    k = k * (1.0 + (a - 1.0) * k_a)
    r_h, k_h, v_h, a_h = hs(r), hs(k), hs(v), hs(a)
    y, S = wkv_scan(S0.astype(jnp.float32), r_h, hs(decay), kk, kk * a_h, v_h, k_h)
    m = jnp.mean(y, axis=-1, keepdims=True)
    var = jnp.mean(jnp.square(y - m), axis=-1, keepdims=True)
    y = ((y - m) * lax.rsqrt(var + GN_EPS)).reshape(B, T, RWKV_WIDTH) * gn_g + gn_b
    bonus = jnp.sum(r_h * k_h * r_k, axis=-1, keepdims=True) * v_h
    y = (y + bonus.reshape(B, T, RWKV_WIDTH)) * g
    return y @ w_out, S


def hier_moe(v, rg_w, rg_b, re_w, re_b, w_gate, w_up, w_down):
    T = v.shape[0]
    lg = v @ rg_w + rg_b
    pg = jax.nn.softmax(lg, axis=-1)
    gsel = jnp.argmax(lg, axis=-1)
    le = (v @ re_w + re_b).reshape(T, N_GROUPS, EXPERTS_PER_GROUP)
    le_sel = jnp.take_along_axis(le, gsel[:, None, None], axis=1)[:, 0]
    top_v, top_i = lax.top_k(le_sel, TOP_K_IN_GROUP)
    gate = jnp.take_along_axis(pg, gsel[:, None], axis=1) * jax.nn.softmax(top_v, axis=-1)
    eidx = gsel[:, None] * EXPERTS_PER_GROUP + top_i
    A = T * TOP_K_IN_GROUP
    NB = -(-A // MOE_BLOCK) + N_EXPERTS
    P = NB * MOE_BLOCK
    e_flat = eidx.reshape(A).astype(jnp.int32)
    tok_flat = jnp.repeat(jnp.arange(T, dtype=jnp.int32), TOP_K_IN_GROUP)
    order = jnp.argsort(e_flat)
    e_sorted, tok_sorted, gate_sorted = e_flat[order], tok_flat[order], gate.reshape(A)[order]
    counts = jnp.zeros((N_EXPERTS,), jnp.int32).at[e_flat].add(1)
    padded = ((counts + MOE_BLOCK - 1) // MOE_BLOCK) * MOE_BLOCK
    ends_padded = jnp.cumsum(padded)
    starts_padded = ends_padded - padded
    starts = jnp.cumsum(counts) - counts
    dest = starts_padded[e_sorted] + (jnp.arange(A, dtype=jnp.int32) - starts[e_sorted])
    row_tok = jnp.full((P,), T, jnp.int32).at[dest].set(tok_sorted)
    row_gate = jnp.zeros((P,), jnp.float32).at[dest].set(gate_sorted)
    block_expert = jnp.clip(jnp.searchsorted(ends_padded, jnp.arange(NB, dtype=jnp.int32) * MOE_BLOCK,
                                             side='right'), 0, N_EXPERTS - 1)
    v_pad = jnp.concatenate([v, jnp.zeros((1, v.shape[1]), v.dtype)], axis=0)
    xb = v_pad[row_tok].reshape(NB, MOE_BLOCK, v.shape[1])

    def expert_block(args):
        xblk, e = args
        h = jax.nn.silu(xblk @ w_gate[e].astype(jnp.float32)) * (xblk @ w_up[e].astype(jnp.float32))
        return h @ w_down[e].astype(jnp.float32)

    yb = lax.map(expert_block, (xb, block_expert)).reshape(P, v.shape[1])
    out = jnp.zeros((T + 1, v.shape[1]), jnp.float32).at[row_tok].add(yb * row_gate[:, None])
    return out[:T]


def layer(x, conv_buf, shift_row, S0, norm1_g, w_in, conv_dw_w, conv_dw_b, conv_ln_g, conv_ln_b,
          conv_out_w, conv_out_b, rwkv_mu, rwkv_w0, rwkv_w2, rwkv_a0, rwkv_a2, rwkv_g2, rwkv_k_k,
          rwkv_k_a, rwkv_r_k, rwkv_gn_g, rwkv_gn_b, rwkv_out_w, w_o, norm2_g, router_group_w,
          router_group_b, router_expert_w, router_expert_b, expert_w_gate, expert_w_up, expert_w_down):
    f32 = lambda t: t.astype(jnp.float32)
    x = f32(x)
    B, T, D = x.shape
    u = rmsnorm(x, norm1_g)
    w_in = f32(w_in)
    p = u @ w_in
    prev_rw = f32(shift_row) @ w_in[:, RW_START:GATE_START]
    yA, new_buf = conformer_conv(p[..., :GLU_COLS], conv_buf, conv_dw_w, f32(conv_dw_b), f32(conv_ln_g),
                                 f32(conv_ln_b), f32(conv_out_w), f32(conv_out_b))
    yB, S = rwkv7_mix(p[..., RW_START:GATE_START], prev_rw, S0, f32(rwkv_mu), f32(rwkv_w0), f32(rwkv_w2),
                      f32(rwkv_a0), f32(rwkv_a2), f32(rwkv_g2), f32(rwkv_k_k), f32(rwkv_k_a), f32(rwkv_r_k),
                      f32(rwkv_gn_g), f32(rwkv_gn_b), f32(rwkv_out_w))
    gates = jax.nn.sigmoid(p[..., GATE_START:])
    x = x + (gates[..., :D] * yA + gates[..., D:] * yB) @ f32(w_o)
    hmoe = rmsnorm(x, norm2_g).reshape(B * T, D)
    x = x + hier_moe(hmoe, f32(router_group_w), f32(router_group_b), f32(router_expert_w),
                     f32(router_expert_b), expert_w_gate, expert_w_up, expert_w_down).reshape(B, T, D)
    return x, new_buf, u[:, -1], S


def setup_inputs(seed: int = 0) -> dict:
    key = jax.random.key(seed)
    ks = jax.random.split(key, 40)
    nrm = lambda i, shape, s: jax.random.normal(ks[i], shape, jnp.float32) * s
    L = DEPTH
    return {
        "x_prompt": nrm(0, (BATCH, SEQ, D_MODEL), 1.0),
        "x_sample": nrm(1, (DEC_BATCH, DEC_SEQ, D_MODEL), 1.0),
        "state_conv": nrm(2, (L, DEC_BATCH, CONV_KERNEL - 1, CONV_WIDTH), 0.5),
        "state_shift": nrm(3, (L, DEC_BATCH, D_MODEL), 1.0),
        "state_wkv": nrm(4, (L, DEC_BATCH, RWKV_HEADS, HEAD_DIM, HEAD_DIM), 0.1),
        "norm1_g": 1.0 + nrm(5, (L, D_MODEL), 0.05),
        "w_in": nrm(6, (L, D_MODEL, IN_COLS), D_MODEL ** -0.5),
        "conv_dw_w": nrm(7, (L, CONV_KERNEL, CONV_WIDTH), CONV_KERNEL ** -0.5),
        "conv_dw_b": nrm(8, (L, CONV_WIDTH), 0.01),
        "conv_ln_g": 1.0 + nrm(9, (L, CONV_WIDTH), 0.05),
        "conv_ln_b": nrm(10, (L, CONV_WIDTH), 0.01),
        "conv_out_w": nrm(11, (L, CONV_WIDTH, D_MODEL), CONV_WIDTH ** -0.5),
        "conv_out_b": nrm(12, (L, D_MODEL), 0.01),
        "rwkv_mu": jax.random.uniform(ks[13], (L, RW_COLS), jnp.float32),
        "rwkv_w0": nrm(14, (L, RWKV_WIDTH), 0.5),
        "rwkv_w2": nrm(15, (L, DECAY_RANK, RWKV_WIDTH), 0.1),
        "rwkv_a0": nrm(16, (L, RWKV_WIDTH), 0.1),
        "rwkv_a2": nrm(17, (L, ICLR_RANK, RWKV_WIDTH), 0.1),
        "rwkv_g2": nrm(18, (L, GATE_RANK, RWKV_WIDTH), GATE_RANK ** -0.5),
        "rwkv_k_k": 0.85 + nrm(19, (L, RWKV_WIDTH), 0.05),
        "rwkv_k_a": 1.0 + nrm(20, (L, RWKV_WIDTH), 0.05),
        "rwkv_r_k": nrm(21, (L, RWKV_HEADS, HEAD_DIM), 0.1),
        "rwkv_gn_g": 1.0 + nrm(22, (L, RWKV_WIDTH), 0.05),
        "rwkv_gn_b": nrm(23, (L, RWKV_WIDTH), 0.01),
        "rwkv_out_w": nrm(24, (L, RWKV_WIDTH, D_MODEL), RWKV_WIDTH ** -0.5),
        "w_o": nrm(25, (L, D_MODEL, D_MODEL), D_MODEL ** -0.5),
        "norm2_g": 1.0 + nrm(26, (L, D_MODEL), 0.05),
        "router_group_w": nrm(27, (L, D_MODEL, N_GROUPS), D_MODEL ** -0.5),
        "router_group_b": nrm(28, (L, N_GROUPS), 0.01),
        "router_expert_w": nrm(29, (L, D_MODEL, N_EXPERTS), D_MODEL ** -0.5),
        "router_expert_b": nrm(30, (L, N_EXPERTS), 0.01),
        "expert_w_gate": nrm(31, (L, N_EXPERTS, D_MODEL, D_EXPERT), D_MODEL ** -0.5),
        "expert_w_up": nrm(32, (L, N_EXPERTS, D_MODEL, D_EXPERT), D_MODEL ** -0.5),
        "expert_w_down": nrm(33, (L, N_EXPERTS, D_EXPERT, D_MODEL), D_EXPERT ** -0.5),
        "final_norm_g": 1.0 + nrm(34, (D_MODEL,), 0.05),
    }


def reference(x_prompt, x_sample, state_conv, state_shift, state_wkv, norm1_g, w_in, conv_dw_w, conv_dw_b,
              conv_ln_g, conv_ln_b, conv_out_w, conv_out_b, rwkv_mu, rwkv_w0, rwkv_w2, rwkv_a0, rwkv_a2,
              rwkv_g2, rwkv_k_k, rwkv_k_a, rwkv_r_k, rwkv_gn_g, rwkv_gn_b, rwkv_out_w, w_o, norm2_g,
              router_group_w, router_group_b, router_expert_w, router_expert_b, expert_w_gate, expert_w_up,
              expert_w_down, final_norm_g):
    B = x_prompt.shape[0]
    hp, hs = x_prompt, x_sample
    conv_p, shift_p, wkv_p, conv_s, shift_s, wkv_s = [], [], [], [], [], []
    for l in range(DEPTH):
        w = (norm1_g[l], w_in[l], conv_dw_w[l], conv_dw_b[l], conv_ln_g[l], conv_ln_b[l], conv_out_w[l],
             conv_out_b[l], rwkv_mu[l], rwkv_w0[l], rwkv_w2[l], rwkv_a0[l], rwkv_a2[l], rwkv_g2[l],
             rwkv_k_k[l], rwkv_k_a[l], rwkv_r_k[l], rwkv_gn_g[l], rwkv_gn_b[l], rwkv_out_w[l], w_o[l],
             norm2_g[l], router_group_w[l], router_group_b[l], router_expert_w[l], router_expert_b[l],
             expert_w_gate[l], expert_w_up[l], expert_w_down[l])
        buf0 = jnp.zeros((B, CONV_KERNEL - 1, CONV_WIDTH), jnp.float32)
        shift0 = jnp.zeros((B, D_MODEL), jnp.float32)
        S0 = jnp.zeros((B, RWKV_HEADS, HEAD_DIM, HEAD_DIM), jnp.float32)
        hp, cb, sr, S = layer(hp, buf0, shift0, S0, *w)
        conv_p.append(cb); shift_p.append(sr); wkv_p.append(S)
        hs, cb, sr, S = layer(hs, state_conv[l], state_shift[l], state_wkv[l], *w)
        conv_s.append(cb); shift_s.append(sr); wkv_s.append(S)
    y_prompt = rmsnorm(hp, final_norm_g).astype(x_prompt.dtype)
    y_sample = rmsnorm(hs, final_norm_g).astype(x_sample.dtype)
    sdt = state_wkv.dtype
    return (y_prompt, y_sample,
            jnp.stack(conv_p).astype(state_conv.dtype), jnp.stack(shift_p).astype(state_shift.dtype),
            jnp.stack(wkv_p).astype(sdt),
            jnp.stack(conv_s).astype(state_conv.dtype), jnp.stack(shift_s).astype(state_shift.dtype),
            jnp.stack(wkv_s).astype(sdt))
```

```python
import functools
import math

import jax
import jax.numpy as jnp
from jax import lax
from jax.experimental import pallas as pl
from jax.experimental.pallas import tpu as pltpu

F32 = jnp.float32
BF16 = jnp.bfloat16

NORM_EPS = 1e-6
LN_EPS = 1e-5
GN_EPS = 64e-5
LANES = 128
MOE_BLOCK = 128
CONV_HALO = 32
VMEM_LIMIT = 56 * 1024 * 1024


def _cparams(*sem):
    return pltpu.CompilerParams(dimension_semantics=sem, vmem_limit_bytes=VMEM_LIMIT)


def _dot(a, b):
    return jnp.dot(a, b, preferred_element_type=F32)


def _dot_nt(a, b):
    return lax.dot_general(a, b, (((1,), (1,)), ((), ())), preferred_element_type=F32)


def _dot_tn(a, b):
    return lax.dot_general(a, b, (((0,), (0,)), ((), ())), preferred_element_type=F32)


def _split2(x):
    hi = x.astype(BF16)
    lo = (x - hi.astype(F32)).astype(BF16)
    return hi, lo


def _sigmoid(x):
    return 1.0 / (1.0 + jnp.exp(-x))


def _col_chunks(n, width=512):
    out, c = [], 0
    while c < n:
        w = min(width, n - c)
        out.append((c, w))
        c += w
    return out


def _inproj_kernel(x_ref, g_ref, w_ref, glu_ref, prw_ref, gate_ref, *, cw, rw, gw):
    x = x_ref[...]
    ms = jnp.mean(x * x, axis=-1, keepdims=True)
    ub = (x * lax.rsqrt(ms + NORM_EPS) * g_ref[...]).astype(BF16)
    for c0, w in _col_chunks(cw):
        a = _dot(ub, w_ref[:, c0:c0 + w])
        gt = _dot(ub, w_ref[:, cw + c0:cw + c0 + w])
        glu_ref[:, c0:c0 + w] = a * _sigmoid(gt)
    for c0, w in _col_chunks(rw):
        prw_ref[:, c0:c0 + w] = _dot(ub, w_ref[:, 2 * cw + c0:2 * cw + c0 + w]).astype(BF16)
    for c0, w in _col_chunks(gw):
        s = 2 * cw + rw + c0
        gate_ref[:, c0:c0 + w] = _sigmoid(_dot(ub, w_ref[:, s:s + w])).astype(BF16)


def _inproj(x2, g1, w_in_b, cw, rw, gw, tm):
    m, d = x2.shape
    ncol = w_in_b.shape[1]
    return pl.pallas_call(
        functools.partial(_inproj_kernel, cw=cw, rw=rw, gw=gw),
        out_shape=(jax.ShapeDtypeStruct((m, cw), F32),
                   jax.ShapeDtypeStruct((m, rw), BF16),
                   jax.ShapeDtypeStruct((m, gw), BF16)),
        grid=(m // tm,),
        in_specs=[pl.BlockSpec((tm, d), lambda i: (i, 0)),
                  pl.BlockSpec((1, d), lambda i: (0, 0)),
                  pl.BlockSpec((d, ncol), lambda i: (0, 0), pipeline_mode=pl.Buffered(1))],
        out_specs=(pl.BlockSpec((tm, cw), lambda i: (i, 0)),
                   pl.BlockSpec((tm, rw), lambda i: (i, 0)),
                   pl.BlockSpec((tm, gw), lambda i: (i, 0))),
        compiler_params=_cparams("parallel"),
        name="inproj",
    )(x2, g1, w_in_b)


def _rowproj_kernel(x_ref, w_ref, o_ref):
    o_ref[...] = _dot(x_ref[...].astype(BF16), w_ref[...])


def _rowproj(x2, w_b):
    m, d = x2.shape
    n = w_b.shape[1]
    return pl.pallas_call(
        _rowproj_kernel,
        out_shape=jax.ShapeDtypeStruct((m, n), F32),
        grid=(1,),
        in_specs=[pl.BlockSpec((m, d), lambda i: (0, 0)), pl.BlockSpec((d, n), lambda i: (0, 0))],
        out_specs=pl.BlockSpec((m, n), lambda i: (0, 0)),
        compiler_params=_cparams("arbitrary"),
        name="shift_rowproj",
    )(x2, w_b)


def _rmsnorm_rows_kernel(x_ref, g_ref, o_ref):
    x = x_ref[...]
    ms = jnp.mean(x * x, axis=-1, keepdims=True)
    o_ref[...] = x * lax.rsqrt(ms + NORM_EPS) * g_ref[...]


def _rmsnorm_rows(x2, g1):
    m, d = x2.shape
    return pl.pallas_call(
        _rmsnorm_rows_kernel,
        out_shape=jax.ShapeDtypeStruct((m, d), F32),
        grid=(1,),
        in_specs=[pl.BlockSpec((m, d), lambda i: (0, 0)), pl.BlockSpec((1, d), lambda i: (0, 0))],
        out_specs=pl.BlockSpec((m, d), lambda i: (0, 0)),
        compiler_params=_cparams("arbitrary"),
        name="shift_rmsnorm",
    )(x2, g1)


def _conv_kernel(glu_ref, halo_ref, gate_ref, dw_ref, dwb_ref, lng_ref, lnb_ref, ow_ref, ob_ref,
                 o_ref, full_ref, *, taps):
    nbk, tt, cw = glu_ref.shape
    full_ref[:, 0:CONV_HALO, :] = halo_ref[...]
    full_ref[:, CONV_HALO:CONV_HALO + tt, :] = glu_ref[...]
    first = CONV_HALO - (taps - 1)
    acc = jnp.zeros((nbk, tt, cw), F32) + dwb_ref[...]
    for j in range(taps):
        acc = acc + full_ref[:, first + j:first + j + tt, :] * dw_ref[j:j + 1, :]
    mean = jnp.mean(acc, axis=-1, keepdims=True)
    cen = acc - mean
    var = jnp.mean(cen * cen, axis=-1, keepdims=True)
    y = cen * lax.rsqrt(var + LN_EPS) * lng_ref[...] + lnb_ref[...]
    y = y * _sigmoid(y)
    y2 = y.reshape(nbk * tt, cw).astype(BF16)
    out = _dot(y2, ow_ref[...]) + ob_ref[...]
    gate = gate_ref[...].reshape(nbk * tt, -1).astype(F32)
    o_ref[...] = (out * gate).reshape(o_ref.shape).astype(o_ref.dtype)


def _conv_branch(glu3, halo3, gates3, dw, dwb, lng, lnb, ow_b, ob, nbk):
    nb, tt, cw = glu3.shape
    d = ow_b.shape[1]
    taps = dw.shape[0]
    full = lambda shape: pl.BlockSpec(shape, lambda i: tuple(0 for _ in shape))
    return pl.pallas_call(
        functools.partial(_conv_kernel, taps=taps),
        out_shape=jax.ShapeDtypeStruct((nb, tt, d), BF16),
        grid=(nb // nbk,),
        in_specs=[pl.BlockSpec((nbk, tt, cw), lambda i: (i, 0, 0)),
                  pl.BlockSpec((nbk, CONV_HALO, cw), lambda i: (i, 0, 0)),
                  pl.BlockSpec((nbk, tt, d), lambda i: (i, 0, 0)),
                  full(dw.shape), full(dwb.shape), full(lng.shape), full(lnb.shape),
                  full(ow_b.shape), full(ob.shape)],
        out_specs=pl.BlockSpec((nbk, tt, d), lambda i: (i, 0, 0)),
        scratch_shapes=[pltpu.VMEM((nbk, CONV_HALO + tt, cw), F32)],
        compiler_params=_cparams("parallel"),
        name="conv_branch",
    )(glu3, halo3, gates3, dw, dwb, lng, lnb, ow_b, ob)


def _prep_kernel(p_ref, prev_ref, mu_ref, w0_ref, w2_ref, a0_ref, a2_ref, g2_ref, kk_ref, ka_ref,
                 hsum_ref, hexp_ref,
                 r_ref, lw_ref, kn_ref, b_ref, v_ref, k_ref, g_ref, sh_ref, *, d):
    nbk, tt, rw = p_ref.shape
    sh_ref[:, 8:8 + tt, :] = p_ref[...].astype(F32)
    sh_ref[:, 7:8, :] = prev_ref[...]
    p = sh_ref[:, 8:8 + tt, :]
    pprev = sh_ref[:, 7:7 + tt, :]
    xm = (p + mu_ref[...] * (pprev - p)).reshape(nbk * tt, rw)
    r = xm[:, 0:d]
    k = xm[:, d:2 * d]
    v = xm[:, 2 * d:3 * d]
    wa = xm[:, 3 * d:3 * d + LANES]
    gd = xm[:, 3 * d + LANES:3 * d + 2 * LANES]
    lw = _dot(jnp.tanh(wa).astype(BF16), w2_ref[...])
    la = _dot(wa.astype(BF16), a2_ref[...])
    logw = -math.exp(-0.5) * _sigmoid(w0_ref[...] + lw)
    a = _sigmoid(a0_ref[...] + la)
    g = _dot(_sigmoid(gd).astype(BF16), g2_ref[...])
    kk = k * kk_ref[...]
    sq_hi, sq_lo = _split2(kk * kk)
    ss = _dot(sq_hi, hsum_ref[...]) + _dot(sq_lo, hsum_ref[...])
    inv = 1.0 / jnp.maximum(jnp.sqrt(ss), 1e-12)
    inv_hi, inv_lo = _split2(inv)
    kn = kk * (_dot(inv_hi, hexp_ref[...]) + _dot(inv_lo, hexp_ref[...]))
    r_ref[...] = r.reshape(r_ref.shape)
    lw_ref[...] = logw.reshape(lw_ref.shape)
    kn_ref[...] = kn.reshape(kn_ref.shape)
    b_ref[...] = (kn * a).reshape(b_ref.shape)
    v_ref[...] = v.reshape(v_ref.shape)
    k_ref[...] = (k * (1.0 + (a - 1.0) * ka_ref[...])).reshape(k_ref.shape)
    g_ref[...] = g.reshape(g_ref.shape).astype(g_ref.dtype)


def _rwkv_prep(prw3, prev3, mu, w0, w2p, a0, a2p, g2b, k_k, k_a, hsum, hexp, nbk):
    nb, tt, rw = prw3.shape
    d = w0.shape[1]
    full = lambda a: pl.BlockSpec(a.shape, lambda i: tuple(0 for _ in a.shape))
    tok = pl.BlockSpec((nbk, tt, d), lambda i: (i, 0, 0))
    o32 = jax.ShapeDtypeStruct((nb, tt, d), F32)
    return pl.pallas_call(
        functools.partial(_prep_kernel, d=d),
        out_shape=(o32, o32, o32, o32, o32, o32, jax.ShapeDtypeStruct((nb, tt, d), BF16)),
        grid=(nb // nbk,),
        in_specs=[pl.BlockSpec((nbk, tt, rw), lambda i: (i, 0, 0)),
                  pl.BlockSpec((nbk, 1, rw), lambda i: (i, 0, 0)),
                  full(mu), full(w0), full(w2p), full(a0), full(a2p), full(g2b), full(k_k), full(k_a),
                  full(hsum), full(hexp)],
        out_specs=(tok,) * 7,
        scratch_shapes=[pltpu.VMEM((nbk, 8 + tt, rw), F32)],
        compiler_params=_cparams("parallel"),
        name="rwkv_prep",
    )(prw3, prev3, mu, w0, w2p, a0, a2p, g2b, k_k, k_a, hsum, hexp)


def _tri_inverse(nab, rowi, coli, c):
    base = min(c, 16)
    blk = lambda s: (rowi // s) == (coli // s)
    eye = (rowi == coli).astype(F32)
    nd = jnp.where(blk(base), nab, 0.0)
    p = eye + nd
    x = nd
    for _ in range(int(math.log2(base)) - 1):
        xb = x.astype(BF16)
        x = _dot(xb, xb)
        p = p + _dot(p.astype(BF16), x.astype(BF16))
    s = base
    while s < c:
        off = jnp.where(blk(2 * s) & jnp.logical_not(blk(s)), nab, 0.0)
        pb = p.astype(BF16)
        p = p + _dot(_dot(pb, off.astype(BF16)).astype(BF16), pb)
        s *= 2
    return p


def _wkv_kernel(r_ref, lw_ref, kn_ref, b_ref, v_ref, k_ref, s0_ref, rk_ref, gng_ref, gnb_ref,
                o_ref, so_ref, y1_ref, rp_ref, mx_ref, d1_ref, s_ref, *, c, hd):
    nbk, t, _ = r_ref.shape
    nchunks = t // c
    rows = 2 * c
    lane = lax.broadcasted_iota(jnp.int32, (1, LANES), 1)
    m0 = lane < hd
    rowi = lax.broadcasted_iota(jnp.int32, (rows, rows), 0)
    coli = lax.broadcasted_iota(jnp.int32, (rows, rows), 1)
    same = (rowi // c) == (coli // c)
    strict = same & ((rowi % c) > (coli % c))
    incl = same & ((rowi % c) >= (coli % c))
    tpos = lax.broadcasted_iota(jnp.int32, (c, LANES), 0)
    ki = lax.broadcasted_iota(jnp.int32, (LANES, LANES), 0)
    kj = lax.broadcasted_iota(jnp.int32, (LANES, LANES), 1)
    eye_l = (ki == kj).astype(F32)
    bd_l = (ki // hd) == (kj // hd)

    def stack(x):
        return jnp.concatenate([jnp.where(m0, x, 0.0), jnp.where(m0, 0.0, x)], axis=0)

    for bi in range(nbk):
        def phase1(ci, carry):
            sl = pl.ds(pl.multiple_of(ci * c, c), c)
            lw = lw_ref[bi, sl, :]
            cum = lw
            s = 1
            while s < c:
                cum = cum + jnp.where(tpos >= s, pltpu.roll(cum, s, axis=0), 0.0)
                s *= 2
            g_in = jnp.exp(cum)
            g_inv = jnp.exp(-cum)
            g_prev = jnp.exp(cum - lw)
            g_last = g_in[c - 1:c, :]
            lq = stack(kn_ref[bi, sl, :] * g_prev)
            lr = stack(r_ref[bi, sl, :] * g_in)
            rb = stack(b_ref[bi, sl, :] * g_inv)
            rkk = stack(k_ref[bi, sl, :] * g_inv)
            vs = stack(v_ref[bi, sl, :])
            lhs = jnp.concatenate([lq, lr], axis=0).astype(BF16)
            rhs = jnp.concatenate([rb, rkk], axis=0).astype(BF16)
            amat = _dot_nt(lhs, rhs)
            nab = jnp.where(strict, -amat[0:rows, 0:rows], 0.0)
            nak = jnp.where(strict, -amat[0:rows, rows:2 * rows], 0.0)
            mrb = jnp.where(incl, amat[rows:2 * rows, 0:rows], 0.0)
            mrk = jnp.where(incl, amat[rows:2 * rows, rows:2 * rows], 0.0)
            tinv = _tri_inverse(nab, rowi, coli, c)
            vsb = vs.astype(BF16)
            nakv = _dot(nak.astype(BF16), vsb)
            tx = _dot(tinv.astype(BF16), jnp.concatenate([nakv, lq], axis=1).astype(BF16))
            w1 = tx[:, 0:LANES]
            lqp = tx[:, LANES:2 * LANES]
            mx2 = _dot(mrb.astype(BF16), tx.astype(BF16))
            y1 = _dot(mrk.astype(BF16), vsb) + mx2[:, 0:LANES]
            lrp = lr - mx2[:, LANES:2 * LANES]
            rbb = rb.astype(BF16)
            mx = (eye_l - jnp.where(bd_l, _dot_tn(lqp.astype(BF16), rbb), 0.0)) * g_last
            d1 = jnp.where(bd_l, _dot_tn(w1.astype(BF16), rbb) + _dot_tn(vsb, rkk.astype(BF16)), 0.0) * g_last
            y1_ref[ci] = y1
            rp_ref[ci] = lrp
            mx_ref[ci] = mx
            d1_ref[ci] = d1
            return carry

        lax.fori_loop(0, nchunks, phase1, 0)

        z = jnp.zeros((hd, hd), F32)
        s_ref[...] = jnp.concatenate(
            [jnp.concatenate([s0_ref[bi, 0], z], axis=1),
             jnp.concatenate([z, s0_ref[bi, 1]], axis=1)], axis=0)

        def phase2(ci, carry):
            sl = pl.ds(pl.multiple_of(ci * c, c), c)
            s_hi, s_lo = _split2(s_ref[...])
            rp_hi, rp_lo = _split2(rp_ref[ci])
            mx_hi, mx_lo = _split2(mx_ref[ci])
            yst = y1_ref[ci] + _dot_nt(rp_hi, s_hi) + _dot_nt(rp_lo, s_hi) + _dot_nt(rp_hi, s_lo)
            s_ref[...] = _dot(s_hi, mx_hi) + _dot(s_lo, mx_hi) + _dot(s_hi, mx_lo) + d1_ref[ci]
            hm = jnp.concatenate([jnp.broadcast_to(m0, (c, LANES)),
                                  jnp.broadcast_to(jnp.logical_not(m0), (c, LANES))], axis=0)
            mean = jnp.sum(yst, axis=-1, keepdims=True) * (1.0 / hd)
            cen = jnp.where(hm, yst - mean, 0.0)
            var = jnp.sum(cen * cen, axis=-1, keepdims=True) * (1.0 / hd)
            yn = cen * lax.rsqrt(var + GN_EPS)
            rkv = stack(r_ref[bi, sl, :] * k_ref[bi, sl, :] * rk_ref[...])
            bonus = jnp.sum(rkv, axis=-1, keepdims=True) * stack(v_ref[bi, sl, :])
            yp = yn[0:c] + yn[c:rows]
            bp = bonus[0:c] + bonus[c:rows]
            o_ref[bi, sl, :] = (yp * gng_ref[...] + gnb_ref[...] + bp).astype(o_ref.dtype)
            return carry

        lax.fori_loop(0, nchunks, phase2, 0)
        sf = s_ref[...]
        so_ref[bi, 0] = sf[0:hd, 0:hd]
        so_ref[bi, 1] = sf[hd:2 * hd, hd:2 * hd]


def _wkv(r, lw, kn, b, v, k, s0, r_k, gn_g, gn_b, c, nbk):
    bsz, t, d = r.shape
    hd = s0.shape[-1]
    npair = d // LANES
    nchunks = t // c
    rows = 2 * c
    tok = pl.BlockSpec((nbk, t, LANES), lambda i, j: (i, 0, j))
    st = pl.BlockSpec((nbk, 2, hd, hd), lambda i, j: (i, j, 0, 0))
    vec = pl.BlockSpec((1, LANES), lambda i, j: (0, j))
    return pl.pallas_call(
        functools.partial(_wkv_kernel, c=c, hd=hd),
        out_shape=(jax.ShapeDtypeStruct((bsz, t, d), BF16),
                   jax.ShapeDtypeStruct(s0.shape, F32)),
        grid=(bsz // nbk, npair),
        in_specs=[tok, tok, tok, tok, tok, tok, st, vec, vec, vec],
        out_specs=(tok, st),
        scratch_shapes=[pltpu.VMEM((nchunks, rows, LANES), F32),
                        pltpu.VMEM((nchunks, rows, LANES), F32),
                        pltpu.VMEM((nchunks, LANES, LANES), F32),
                        pltpu.VMEM((nchunks, LANES, LANES), F32),
                        pltpu.VMEM((LANES, LANES), F32)],
        compiler_params=_cparams("parallel", "parallel"),
        name="wkv_scan",
    )(r, lw, kn, b, v, k, s0, r_k, gn_g, gn_b)


def _mix_kernel(o_ref, g_ref, za_ref, gb_ref, x_ref, wout_ref, wo_ref, n2_ref, rwh_ref, rwl_ref, rb_ref,
                x1_ref, h_ref, route_ref, *, n_exp, n_grp):
    og = (o_ref[...].astype(F32) * g_ref[...].astype(F32)).astype(BF16)
    yb = _dot(og, wout_ref[...])
    mix = za_ref[...].astype(F32) + gb_ref[...].astype(F32) * yb
    x1 = x_ref[...] + _dot(mix.astype(BF16), wo_ref[...])
    x1_ref[...] = x1
    ms = jnp.mean(x1 * x1, axis=-1, keepdims=True)
    h = x1 * lax.rsqrt(ms + NORM_EPS) * n2_ref[...]
    h_ref[...] = h
    h_hi, h_lo = _split2(h)
    lg = _dot(h_hi, rwh_ref[...]) + _dot(h_lo, rwh_ref[...]) + _dot(h_hi, rwl_ref[...]) + rb_ref[...]
    lane = lax.broadcasted_iota(jnp.int32, lg.shape, 1)
    big = jnp.int32(1 << 20)
    neg = jnp.float32(-jnp.inf)
    gmask = (lane >= n_exp) & (lane < n_exp + n_grp)
    glog = jnp.where(gmask, lg, neg)
    gmax = jnp.max(glog, axis=-1, keepdims=True)
    gsel = jnp.min(jnp.where(glog == gmax, lane - n_exp, big), axis=-1, keepdims=True)
    pg = 1.0 / jnp.sum(jnp.where(gmask, jnp.exp(lg - gmax), 0.0), axis=-1, keepdims=True)
    epg = n_exp // n_grp
    emask = (lane < n_exp) & ((lane // epg) == gsel)
    el = jnp.where(emask, lg, neg)
    m1 = jnp.max(el, axis=-1, keepdims=True)
    i1 = jnp.min(jnp.where(el == m1, lane, big), axis=-1, keepdims=True)
    el2 = jnp.where(lane == i1, neg, el)
    m2 = jnp.max(el2, axis=-1, keepdims=True)
    i2 = jnp.min(jnp.where(el2 == m2, lane, big), axis=-1, keepdims=True)
    e2 = jnp.exp(m2 - m1)
    p1 = 1.0 / (1.0 + e2)
    p2 = e2 * p1
    route = jnp.where(lane == 0, i1.astype(F32),
                      jnp.where(lane == 1, i2.astype(F32),
                                jnp.where(lane == 2, pg * p1, jnp.where(lane == 3, pg * p2, 0.0))))
    route_ref[...] = route


def _mix(o2, g2, za2, gates2, x2, wout_b, wo_b, n2, rw_hi, rw_lo, rbias, n_exp, n_grp, tm):
    m, d = x2.shape
    tokb = pl.BlockSpec((tm, d), lambda i: (i, 0))
    full = lambda a: pl.BlockSpec(a.shape, lambda i: tuple(0 for _ in a.shape))
    return pl.pallas_call(
        functools.partial(_mix_kernel, n_exp=n_exp, n_grp=n_grp),
        out_shape=(jax.ShapeDtypeStruct((m, d), F32), jax.ShapeDtypeStruct((m, d), F32),
                   jax.ShapeDtypeStruct((m, LANES), F32)),
        grid=(m // tm,),
        in_specs=[tokb, tokb, tokb, pl.BlockSpec((tm, d), lambda i: (i, 1)), tokb,
                  full(wout_b), full(wo_b), full(n2), full(rw_hi), full(rw_lo), full(rbias)],
        out_specs=(tokb, tokb, pl.BlockSpec((tm, LANES), lambda i: (i, 0))),
        compiler_params=_cparams("parallel"),
        name="mix_router",
    )(o2, g2, za2, gates2, x2, wout_b, wo_b, n2, rw_hi, rw_lo, rbias)


def _expert_kernel(be_ref, tok_ref, nused_ref, h_ref, wg_ref, wu_ref, wd_ref, o_ref, buf_ref, sem_ref):
    i = pl.program_id(0)
    nused = nused_ref[0]

    def issue(blk, slot):
        def body(rr, carry):
            tok = tok_ref[blk * MOE_BLOCK + rr]
            pltpu.make_async_copy(h_ref.at[pl.ds(tok, 1)], buf_ref.at[slot, pl.ds(rr, 1)],
                                  sem_ref.at[slot]).start()
            return carry
        lax.fori_loop(0, MOE_BLOCK, body, 0)

    @pl.when(i == 0)
    def _():
        issue(0, 0)

    @pl.when(i + 1 < nused)
    def _():
        issue(i + 1, (i + 1) % 2)

    @pl.when(i < nused)
    def _():
        slot = i % 2
        pltpu.make_async_copy(h_ref.at[pl.ds(0, MOE_BLOCK)], buf_ref.at[slot], sem_ref.at[slot]).wait()
        x = buf_ref[slot].astype(BF16)
        hg = _dot(x, wg_ref[0])
        hu = _dot(x, wu_ref[0])
        act = (hg * _sigmoid(hg) * hu).astype(BF16)
        o_ref[...] = _dot(act, wd_ref[0])

    @pl.when(i >= nused)
    def _():
        o_ref[...] = jnp.zeros(o_ref.shape, o_ref.dtype)


def _experts(block_expert, row_tok, nused, h2, wg_b, wu_b, wd_b):
    m, d = h2.shape
    nb = block_expert.shape[0]
    de = wg_b.shape[-1]
    grid_spec = pltpu.PrefetchScalarGridSpec(
        num_scalar_prefetch=3,
        grid=(nb,),
        in_specs=[pl.BlockSpec(memory_space=pl.ANY),
                  pl.BlockSpec((1, d, de), lambda i, be, tk, nu: (be[i], 0, 0)),
                  pl.BlockSpec((1, d, de), lambda i, be, tk, nu: (be[i], 0, 0)),
                  pl.BlockSpec((1, de, d), lambda i, be, tk, nu: (be[i], 0, 0))],
        out_specs=pl.BlockSpec((MOE_BLOCK, d), lambda i, be, tk, nu: (i, 0)),
        scratch_shapes=[pltpu.VMEM((2, MOE_BLOCK, d), F32), pltpu.SemaphoreType.DMA((2,))],
    )
    return pl.pallas_call(
        _expert_kernel,
        out_shape=jax.ShapeDtypeStruct((nb * MOE_BLOCK, d), F32),
        grid_spec=grid_spec,
        compiler_params=_cparams("arbitrary"),
        name="expert_blocks",
    )(block_expert, row_tok, nused, h2, wg_b, wu_b, wd_b)


def _combine_kernel(dest_ref, yb_ref, x1_ref, route_ref, fg_ref, o_ref, buf_ref, sem_ref, *, tm):
    i = pl.program_id(0)
    n = pl.num_programs(0)

    def issue(blk, slot):
        def body(rr, carry):
            for j in range(2):
                dst = dest_ref[(blk * tm + rr) * 2 + j]
                pltpu.make_async_copy(yb_ref.at[pl.ds(dst, 1)], buf_ref.at[slot, j, pl.ds(rr, 1)],
                                      sem_ref.at[slot]).start()
            return carry
        lax.fori_loop(0, tm, body, 0)

    @pl.when(i == 0)
    def _():
        issue(0, 0)

    @pl.when(i + 1 < n)
    def _():
        issue(i + 1, (i + 1) % 2)

    slot = i % 2
    for j in range(2):
        pltpu.make_async_copy(yb_ref.at[pl.ds(0, tm)], buf_ref.at[slot, j], sem_ref.at[slot]).wait()
    route = route_ref[...]
    x2 = x1_ref[...] + (route[:, 2:3] * buf_ref[slot, 0] + route[:, 3:4] * buf_ref[slot, 1])
    ms = jnp.mean(x2 * x2, axis=-1, keepdims=True)
    o_ref[...] = x2 * lax.rsqrt(ms + NORM_EPS) * fg_ref[...]


def _combine(dest, yb, x1, route, fg, tm):
    m, d = x1.shape
    grid_spec = pltpu.PrefetchScalarGridSpec(
        num_scalar_prefetch=1,
        grid=(m // tm,),
        in_specs=[pl.BlockSpec(memory_space=pl.ANY),
                  pl.BlockSpec((tm, d), lambda i, ds: (i, 0)),
                  pl.BlockSpec((tm, LANES), lambda i, ds: (i, 0)),
                  pl.BlockSpec((1, d), lambda i, ds: (0, 0))],
        out_specs=pl.BlockSpec((tm, d), lambda i, ds: (i, 0)),
        scratch_shapes=[pltpu.VMEM((2, 2, tm, d), F32), pltpu.SemaphoreType.DMA((2,))],
    )
    return pl.pallas_call(
        functools.partial(_combine_kernel, tm=tm),
        out_shape=jax.ShapeDtypeStruct((m, d), F32),
        grid_spec=grid_spec,
        compiler_params=_cparams("arbitrary"),
        name="moe_combine",
    )(dest, yb, x1, route, fg)


def _dispatch_indices(route, n_exp):
    m = route.shape[0]
    e_flat = route[:, 0:2].astype(jnp.int32).reshape(2 * m)
    a = 2 * m
    nb = -(-a // MOE_BLOCK) + n_exp
    onehot = (e_flat[:, None] == jnp.arange(n_exp, dtype=jnp.int32)[None, :]).astype(jnp.int32)
    csum = jnp.cumsum(onehot, axis=0)
    rank = jnp.sum((csum - 1) * onehot, axis=1)
    counts = csum[-1]
    padded = ((counts + MOE_BLOCK - 1) // MOE_BLOCK) * MOE_BLOCK
    ends = jnp.cumsum(padded)
    starts = ends - padded
    dest = starts[e_flat] + rank
    tok_flat = jnp.repeat(jnp.arange(m, dtype=jnp.int32), 2)
    row_tok = jnp.zeros((nb * MOE_BLOCK,), jnp.int32).at[dest].set(tok_flat)
    block_expert = jnp.clip(jnp.searchsorted(ends, jnp.arange(nb, dtype=jnp.int32) * MOE_BLOCK, side='right'),
                            0, n_exp - 1).astype(jnp.int32)
    nused = (ends[-1] // MOE_BLOCK).astype(jnp.int32).reshape(1)
    return block_expert, row_tok, nused, dest.astype(jnp.int32)


def _pick_tile(t, pref):
    tile = min(t, pref)
    assert t % tile == 0, (t, tile)
    return tile


def _group_rows(nb, tt, target_rows):
    nbk = max(1, min(nb, target_rows // tt))
    while nb % nbk:
        nbk -= 1
    return nbk


def _layer(x3, conv_buf, shift_row, s0, w, final_g):
    bsz, t, d = x3.shape
    m = bsz * t
    cw, rw, gw = w["cw"], w["rw"], w["gw"]
    taps = w["dw"].shape[0]
    x2 = x3.reshape(m, d)
    tm = _pick_tile(m, 256)
    glu, prw, gates = _inproj(x2, w["n1"], w["w_in_b"], cw, rw, gw, tm)

    tt = _pick_tile(t, 256)
    nt = t // tt
    nbk = _group_rows(bsz * nt, tt, 256)
    glu4 = glu.reshape(bsz, nt, tt, cw)
    buf_pad = jnp.pad(conv_buf.astype(F32), ((0, 0), (CONV_HALO - (taps - 1), 0), (0, 0)))
    if nt > 1:
        assert tt >= CONV_HALO
        halo = jnp.concatenate([buf_pad[:, None], glu4[:, :-1, tt - CONV_HALO:, :]], axis=1)
    else:
        halo = buf_pad[:, None]
    za = _conv_branch(glu4.reshape(bsz * nt, tt, cw), halo.reshape(bsz * nt, CONV_HALO, cw),
                      gates.reshape(bsz * nt, tt, gw), w["dw"], w["dwb"], w["lng"], w["lnb"],
                      w["conv_out_b16"], w["conv_out_bias"], nbk)
    full_seq = jnp.concatenate([conv_buf.astype(F32), glu.reshape(bsz, t, cw)], axis=1)
    new_buf = full_seq[:, -(taps - 1):]

    prev_rw = _rowproj(shift_row.astype(F32), w["w_rw_b"])
    prw4 = prw.reshape(bsz, nt, tt, rw)
    if nt > 1:
        prev = jnp.concatenate([prev_rw[:, None], prw4[:, :-1, tt - 1, :].astype(F32)], axis=1)
    else:
        prev = prev_rw[:, None]
    r, lw, kn, b, v, k, g = _rwkv_prep(prw4.reshape(bsz * nt, tt, rw), prev.reshape(bsz * nt, 1, rw),
                                       w["mu"], w["w0"], w["w2p"], w["a0"], w["a2p"], w["g2b"],
                                       w["k_k"], w["k_a"], w["hsum"], w["hexp"], nbk)
    sh = lambda a: a.reshape(bsz, t, d)
    chunk = _pick_tile(t, 64)
    wkv_nbk = _group_rows(bsz, t, 64)
    o, s_new = _wkv(sh(r), sh(lw), sh(kn), sh(b), sh(v), sh(k), s0.astype(F32), w["r_k"], w["gn_g"],
                    w["gn_b"], chunk, wkv_nbk)

    x1, h, route = _mix(o.reshape(m, d), g.reshape(m, d), za.reshape(m, d), gates, x2, w["rwkv_out_b16"],
                        w["w_o_b16"], w["n2"], w["rt_hi"], w["rt_lo"], w["rt_b"], w["n_exp"], w["n_grp"], tm)
    block_expert, row_tok, nused, dest = _dispatch_indices(route, w["n_exp"])
    yb = _experts(block_expert, row_tok, nused, h, w["wg_b"], w["wu_b"], w["wd_b"])
    y = _combine(dest, yb, x1, route, final_g, _pick_tile(m, 128))
    u_last = _rmsnorm_rows(x3[:, -1, :], w["n1"])
    return y.reshape(bsz, t, d), new_buf, u_last, s_new


def kernel(x_prompt, x_sample, state_conv, state_shift, state_wkv, norm1_g, w_in, conv_dw_w, conv_dw_b,
           conv_ln_g, conv_ln_b, conv_out_w, conv_out_b, rwkv_mu, rwkv_w0, rwkv_w2, rwkv_a0, rwkv_a2,
           rwkv_g2, rwkv_k_k, rwkv_k_a, rwkv_r_k, rwkv_gn_g, rwkv_gn_b, rwkv_out_w, w_o, norm2_g,
           router_group_w, router_group_b, router_expert_w, router_expert_b, expert_w_gate, expert_w_up,
           expert_w_down, final_norm_g):
    depth = norm1_g.shape[0]
    assert depth == 1, "single-layer trunk"
    l = 0
    d = x_prompt.shape[-1]
    cw = conv_out_w.shape[1]
    hd = state_wkv.shape[-1]
    heads = state_wkv.shape[2]
    assert heads * hd == d and 2 * hd == LANES
    dr, ir, gr = rwkv_w2.shape[1], rwkv_a2.shape[1], rwkv_g2.shape[1]
    assert dr + ir == LANES and gr == LANES
    rw = 3 * d + dr + ir + gr
    gw = 2 * d
    n_grp = router_group_w.shape[-1]
    n_exp = router_expert_w.shape[-1]
    row = lambda a: a.astype(F32).reshape(1, -1)

    head_of_lane = jnp.arange(d, dtype=jnp.int32) // hd
    hsum = (head_of_lane[:, None] == jnp.arange(LANES, dtype=jnp.int32)[None, :]).astype(BF16)
    rt_w = jnp.zeros((d, LANES), F32).at[:, :n_exp].set(router_expert_w[l]).at[:, n_exp:n_exp + n_grp].set(
        router_group_w[l])
    rt_hi = rt_w.astype(BF16)
    rt_lo = (rt_w - rt_hi.astype(F32)).astype(BF16)
    rt_b = jnp.zeros((1, LANES), F32).at[0, :n_exp].set(router_expert_b[l]).at[0, n_exp:n_exp + n_grp].set(
        router_group_b[l])
    w_in_b = w_in[l].astype(BF16)
    w = dict(
        cw=cw, rw=rw, gw=gw, n_exp=n_exp, n_grp=n_grp,
        n1=row(norm1_g[l]), w_in_b=w_in_b, w_rw_b=w_in_b[:, 2 * cw:2 * cw + rw],
        dw=conv_dw_w[l].astype(F32), dwb=row(conv_dw_b[l]), lng=row(conv_ln_g[l]), lnb=row(conv_ln_b[l]),
        conv_out_b16=conv_out_w[l].astype(BF16), conv_out_bias=row(conv_out_b[l]),
        mu=row(rwkv_mu[l]), w0=row(rwkv_w0[l]), a0=row(rwkv_a0[l]),
        w2p=jnp.concatenate([rwkv_w2[l], jnp.zeros((ir, d), F32)], axis=0).astype(BF16),
        a2p=jnp.concatenate([jnp.zeros((dr, d), F32), rwkv_a2[l]], axis=0).astype(BF16),
        g2b=rwkv_g2[l].astype(BF16), k_k=row(rwkv_k_k[l]), k_a=row(rwkv_k_a[l]),
        hsum=hsum, hexp=hsum.T, r_k=row(rwkv_r_k[l]), gn_g=row(rwkv_gn_g[l]), gn_b=row(rwkv_gn_b[l]),
        rwkv_out_b16=rwkv_out_w[l].astype(BF16), w_o_b16=w_o[l].astype(BF16), n2=row(norm2_g[l]),
        rt_hi=rt_hi, rt_lo=rt_lo, rt_b=rt_b,
        wg_b=expert_w_gate[l].astype(BF16), wu_b=expert_w_up[l].astype(BF16), wd_b=expert_w_down[l].astype(BF16),
    )
    fg = row(final_norm_g)
    bp = x_prompt.shape[0]
    taps = conv_dw_w.shape[1]
    yp, cbp, srp, sp = _layer(x_prompt.astype(F32), jnp.zeros((bp, taps - 1, cw), F32), jnp.zeros((bp, d), F32),
                              jnp.zeros((bp, heads, hd, hd), F32), w, fg)
    ys, cbs, srs, ss = _layer(x_sample.astype(F32), state_conv[l], state_shift[l], state_wkv[l], w, fg)
    sdt = state_wkv.dtype
    return (yp.astype(x_prompt.dtype), ys.astype(x_sample.dtype),
            cbp[None].astype(state_conv.dtype), srp[None].astype(state_shift.dtype), sp[None].astype(sdt),
            cbs[None].astype(state_conv.dtype), srs[None].astype(state_shift.dtype), ss[None].astype(sdt))
```

```python
import functools
import math

import jax
import jax.numpy as jnp
from jax import lax
from jax.experimental import pallas as pl
from jax.experimental.pallas import tpu as pltpu

F32 = jnp.float32
BF16 = jnp.bfloat16

NORM_EPS = 1e-6
LN_EPS = 1e-5
GN_EPS = 64e-5
LANES = 128
MOE_BLOCK = 128
CONV_HALO = 32
VMEM_LIMIT = 56 * 1024 * 1024
WKV_CHUNK = 64
WKV_TIME_BLOCK = 1024
WKV_PAIRS = 2
WKV_UNITS = 8


def _cparams(*sem):
    return pltpu.CompilerParams(dimension_semantics=sem, vmem_limit_bytes=VMEM_LIMIT)


def _dot(a, b):
    return jnp.dot(a, b, preferred_element_type=F32)


def _dot_nt(a, b):
    return lax.dot_general(a, b, (((1,), (1,)), ((), ())), preferred_element_type=F32)


def _dot_tn(a, b):
    return lax.dot_general(a, b, (((0,), (0,)), ((), ())), preferred_element_type=F32)


def _split2(x):
    hi = x.astype(BF16)
    lo = (x - hi.astype(F32)).astype(BF16)
    return hi, lo


def _sigmoid(x):
    return 1.0 / (1.0 + jnp.exp(-x))


def _col_chunks(n, width=512):
    out, c = [], 0
    while c < n:
        w = min(width, n - c)
        out.append((c, w))
        c += w
    return out


def _inproj_kernel(x_ref, g_ref, w_ref, glu_ref, prw_ref, gate_ref, *, cw, rw, gw):
    x = x_ref[...]
    ms = jnp.mean(x * x, axis=-1, keepdims=True)
    ub = (x * lax.rsqrt(ms + NORM_EPS) * g_ref[...]).astype(BF16)
    for c0, w in _col_chunks(cw):
        a = _dot(ub, w_ref[:, c0:c0 + w])
        gt = _dot(ub, w_ref[:, cw + c0:cw + c0 + w])
        glu_ref[:, c0:c0 + w] = a * _sigmoid(gt)
    for c0, w in _col_chunks(rw):
        prw_ref[:, c0:c0 + w] = _dot(ub, w_ref[:, 2 * cw + c0:2 * cw + c0 + w]).astype(BF16)
    for c0, w in _col_chunks(gw):
        s = 2 * cw + rw + c0
        gate_ref[:, c0:c0 + w] = _sigmoid(_dot(ub, w_ref[:, s:s + w])).astype(BF16)


def _inproj(x2, g1, w_in_b, cw, rw, gw, tm):
    m, d = x2.shape
    ncol = w_in_b.shape[1]
    return pl.pallas_call(
        functools.partial(_inproj_kernel, cw=cw, rw=rw, gw=gw),
        out_shape=(jax.ShapeDtypeStruct((m, cw), F32),
                   jax.ShapeDtypeStruct((m, rw), BF16),
                   jax.ShapeDtypeStruct((m, gw), BF16)),
        grid=(m // tm,),
        in_specs=[pl.BlockSpec((tm, d), lambda i: (i, 0)),
                  pl.BlockSpec((1, d), lambda i: (0, 0)),
                  pl.BlockSpec((d, ncol), lambda i: (0, 0), pipeline_mode=pl.Buffered(1))],
        out_specs=(pl.BlockSpec((tm, cw), lambda i: (i, 0)),
                   pl.BlockSpec((tm, rw), lambda i: (i, 0)),
                   pl.BlockSpec((tm, gw), lambda i: (i, 0))),
        compiler_params=_cparams("parallel"),
        name="inproj",
    )(x2, g1, w_in_b)


def _rowproj_kernel(x_ref, w_ref, o_ref):
    o_ref[...] = _dot(x_ref[...].astype(BF16), w_ref[...])


def _rowproj(x2, w_b):
    m, d = x2.shape
    n = w_b.shape[1]
    return pl.pallas_call(
        _rowproj_kernel,
        out_shape=jax.ShapeDtypeStruct((m, n), F32),
        grid=(1,),
        in_specs=[pl.BlockSpec((m, d), lambda i: (0, 0)), pl.BlockSpec((d, n), lambda i: (0, 0))],
        out_specs=pl.BlockSpec((m, n), lambda i: (0, 0)),
        compiler_params=_cparams("arbitrary"),
        name="shift_rowproj",
    )(x2, w_b)


def _rmsnorm_rows_kernel(x_ref, g_ref, o_ref):
    x = x_ref[...]
    ms = jnp.mean(x * x, axis=-1, keepdims=True)
    o_ref[...] = x * lax.rsqrt(ms + NORM_EPS) * g_ref[...]


def _rmsnorm_rows(x2, g1):
    m, d = x2.shape
    return pl.pallas_call(
        _rmsnorm_rows_kernel,
        out_shape=jax.ShapeDtypeStruct((m, d), F32),
        grid=(1,),
        in_specs=[pl.BlockSpec((m, d), lambda i: (0, 0)), pl.BlockSpec((1, d), lambda i: (0, 0))],
        out_specs=pl.BlockSpec((m, d), lambda i: (0, 0)),
        compiler_params=_cparams("arbitrary"),
        name="shift_rmsnorm",
    )(x2, g1)


def _conv_kernel(glu_ref, halo_ref, gate_ref, dw_ref, dwb_ref, lng_ref, lnb_ref, ow_ref, ob_ref,
                 o_ref, full_ref, *, taps):
    nbk, tt, cw = glu_ref.shape
    full_ref[:, 0:CONV_HALO, :] = halo_ref[...]
    full_ref[:, CONV_HALO:CONV_HALO + tt, :] = glu_ref[...]
    first = CONV_HALO - (taps - 1)
    acc = jnp.zeros((nbk, tt, cw), F32) + dwb_ref[...]
    for j in range(taps):
        acc = acc + full_ref[:, first + j:first + j + tt, :] * dw_ref[j:j + 1, :]
    mean = jnp.mean(acc, axis=-1, keepdims=True)
    cen = acc - mean
    var = jnp.mean(cen * cen, axis=-1, keepdims=True)
    y = cen * lax.rsqrt(var + LN_EPS) * lng_ref[...] + lnb_ref[...]
    y = y * _sigmoid(y)
    y2 = y.reshape(nbk * tt, cw).astype(BF16)
    out = _dot(y2, ow_ref[...]) + ob_ref[...]
    gate = gate_ref[...].reshape(nbk * tt, -1).astype(F32)
    o_ref[...] = (out * gate).reshape(o_ref.shape).astype(o_ref.dtype)


def _conv_branch(glu3, halo3, gates3, dw, dwb, lng, lnb, ow_b, ob, nbk):
    nb, tt, cw = glu3.shape
    d = ow_b.shape[1]
    taps = dw.shape[0]
    full = lambda shape: pl.BlockSpec(shape, lambda i: tuple(0 for _ in shape))
    return pl.pallas_call(
        functools.partial(_conv_kernel, taps=taps),
        out_shape=jax.ShapeDtypeStruct((nb, tt, d), BF16),
        grid=(nb // nbk,),
        in_specs=[pl.BlockSpec((nbk, tt, cw), lambda i: (i, 0, 0)),
                  pl.BlockSpec((nbk, CONV_HALO, cw), lambda i: (i, 0, 0)),
                  pl.BlockSpec((nbk, tt, d), lambda i: (i, 0, 0)),
                  full(dw.shape), full(dwb.shape), full(lng.shape), full(lnb.shape),
                  full(ow_b.shape), full(ob.shape)],
        out_specs=pl.BlockSpec((nbk, tt, d), lambda i: (i, 0, 0)),
        scratch_shapes=[pltpu.VMEM((nbk, CONV_HALO + tt, cw), F32)],
        compiler_params=_cparams("parallel"),
        name="conv_branch",
    )(glu3, halo3, gates3, dw, dwb, lng, lnb, ow_b, ob)


def _prep_kernel(p_ref, prev_ref, mu_ref, w0_ref, w2_ref, a0_ref, a2_ref, g2_ref, kk_ref, ka_ref,
                 hsum_ref, hexp_ref,
                 r_ref, lw_ref, kn_ref, b_ref, v_ref, k_ref, g_ref, sh_ref, *, d):
    nbk, tt, rw = p_ref.shape
    sh_ref[:, 8:8 + tt, :] = p_ref[...].astype(F32)
    sh_ref[:, 7:8, :] = prev_ref[...]
    p = sh_ref[:, 8:8 + tt, :]
    pprev = sh_ref[:, 7:7 + tt, :]
    xm = (p + mu_ref[...] * (pprev - p)).reshape(nbk * tt, rw)
    r = xm[:, 0:d]
    k = xm[:, d:2 * d]
    v = xm[:, 2 * d:3 * d]
    wa = xm[:, 3 * d:3 * d + LANES]
    gd = xm[:, 3 * d + LANES:3 * d + 2 * LANES]
    lw = _dot(jnp.tanh(wa).astype(BF16), w2_ref[...])
    la = _dot(wa.astype(BF16), a2_ref[...])
    logw = -math.exp(-0.5) * _sigmoid(w0_ref[...] + lw)
    a = _sigmoid(a0_ref[...] + la)
    g = _dot(_sigmoid(gd).astype(BF16), g2_ref[...])
    kk = k * kk_ref[...]
    sq_hi, sq_lo = _split2(kk * kk)
    ss = _dot(sq_hi, hsum_ref[...]) + _dot(sq_lo, hsum_ref[...])
    inv = 1.0 / jnp.maximum(jnp.sqrt(ss), 1e-12)
    inv_hi, inv_lo = _split2(inv)
    kn = kk * (_dot(inv_hi, hexp_ref[...]) + _dot(inv_lo, hexp_ref[...]))
    put = lambda ref, val: ref.__setitem__(Ellipsis, val.reshape(ref.shape).astype(ref.dtype))
    put(r_ref, r)
    put(lw_ref, logw)
    put(kn_ref, kn)
    put(b_ref, kn * a)
    put(v_ref, v)
    put(k_ref, k * (1.0 + (a - 1.0) * ka_ref[...]))
    put(g_ref, g)


def _rwkv_prep(prw3, prev3, mu, w0, w2p, a0, a2p, g2b, k_k, k_a, hsum, hexp, nbk):
    nb, tt, rw = prw3.shape
    d = w0.shape[1]
    full = lambda a: pl.BlockSpec(a.shape, lambda i: tuple(0 for _ in a.shape))
    tok = pl.BlockSpec((nbk, tt, d), lambda i: (i, 0, 0))
    o32 = jax.ShapeDtypeStruct((nb, tt, d), F32)
    o16 = jax.ShapeDtypeStruct((nb, tt, d), BF16)
    return pl.pallas_call(
        functools.partial(_prep_kernel, d=d),
        out_shape=(o16, o32, o16, o16, o16, o16, o16),
        grid=(nb // nbk,),
        in_specs=[pl.BlockSpec((nbk, tt, rw), lambda i: (i, 0, 0)),
                  pl.BlockSpec((nbk, 1, rw), lambda i: (i, 0, 0)),
                  full(mu), full(w0), full(w2p), full(a0), full(a2p), full(g2b), full(k_k), full(k_a),
                  full(hsum), full(hexp)],
        out_specs=(tok,) * 7,
        scratch_shapes=[pltpu.VMEM((nbk, 8 + tt, rw), F32)],
        compiler_params=_cparams("parallel"),
        name="rwkv_prep",
    )(prw3, prev3, mu, w0, w2p, a0, a2p, g2b, k_k, k_a, hsum, hexp)


def _wkv_base(c):
    return min(c, 16)


def _wkv_masks(c):
    rows = 2 * c
    ri = jnp.arange(rows, dtype=jnp.int32)[:, None]
    ci = jnp.arange(rows, dtype=jnp.int32)[None, :]
    same = (ri // c) == (ci // c)
    blk = lambda s: (ri // s) == (ci // s)
    out = [-(same & ((ri % c) > (ci % c))).astype(F32), (same & ((ri % c) >= (ci % c))).astype(F32),
           (ri == ci).astype(F32), blk(_wkv_base(c)).astype(F32)]
    s = _wkv_base(c)
    while s < c:
        out.append((blk(2 * s) & jnp.logical_not(blk(s))).astype(F32))
        s *= 2
    return jnp.stack(out)


def _tri_inverse(nabs, cst_ref, c):
    base = _wkv_base(c)
    n = range(len(nabs))
    nds = [nabs[i] * cst_ref[3] for i in n]
    ps = [cst_ref[2] + nds[i] for i in n]
    xbs = [nds[i].astype(BF16) for i in n]
    for _ in range(int(math.log2(base)) - 1):
        xbs = [_dot(xbs[i], xbs[i]).astype(BF16) for i in n]
        ps = [ps[i] + _dot(ps[i].astype(BF16), xbs[i]) for i in n]
    s, lvl = base, 4
    while s < c:
        pbs = [ps[i].astype(BF16) for i in n]
        ts = [_dot(pbs[i], (nabs[i] * cst_ref[lvl]).astype(BF16)).astype(BF16) for i in n]
        ps = [ps[i] + _dot(ts[i], pbs[i]) for i in n]
        s *= 2
        lvl += 1
    return ps


def _wkv_kernel(r_ref, lw_ref, kn_ref, b_ref, v_ref, k_ref, s0_ref, rk_ref, gng_ref, gnb_ref, cst_ref, eye_ref,
                o_ref, so_ref, y1_ref, rp_ref, mxh_ref, mxl_ref, d1_ref, sh_ref, s_ref, *, c, hd, unroll):
    nbk, tb, width = r_ref.shape
    npl = width // LANES
    nch = tb // c
    rows = 2 * c
    ti = pl.program_id(2)
    lane = lax.broadcasted_iota(jnp.int32, (1, LANES), 1)
    m0 = (lane < hd).astype(F32)
    m1 = 1.0 - m0
    tpos = lax.broadcasted_iota(jnp.int32, (c, LANES), 0)
    units = [(bi, p) for bi in range(nbk) for p in range(npl)]

    def stack(x):
        return jnp.concatenate([x * m0, x * m1], axis=0)

    def rows_of(ci):
        if isinstance(ci, int):
            return pl.ds(ci * c, c)
        return pl.ds(pl.multiple_of(ci * c, c), c)

    @pl.when(ti == 0)
    def _():
        z = jnp.zeros((hd, hd), F32)
        for u, (bi, p) in enumerate(units):
            s_ref[u] = jnp.concatenate(
                [jnp.concatenate([s0_ref[bi, 2 * p], z], axis=1),
                 jnp.concatenate([z, s0_ref[bi, 2 * p + 1]], axis=1)], axis=0)

    def phase1(group):
        n = range(len(group))

        def prologue(bi, p, ci):
            sl = rows_of(ci)
            ln = slice(p * LANES, (p + 1) * LANES)
            ld = lambda ref: ref[bi, sl, ln].astype(F32)
            lw = lw_ref[bi, sl, ln]
            cum = lw
            s = 1
            while s < c:
                cum = cum + jnp.where(tpos >= s, pltpu.roll(cum, s, axis=0), 0.0)
                s *= 2
            g_in = jnp.exp(cum)
            g_inv = jnp.exp(-cum)
            g_prev = jnp.exp(cum - lw)
            lq = stack(ld(kn_ref) * g_prev)
            lr = stack(ld(r_ref) * g_in)
            rbb = stack(ld(b_ref) * g_inv).astype(BF16)
            rkb = stack(ld(k_ref) * g_inv).astype(BF16)
            vsb = stack(ld(v_ref)).astype(BF16)
            return dict(g_last=g_in[c - 1:c, :], lq=lq, lr=lr, rbb=rbb, vsb=vsb,
                        lhs=jnp.concatenate([lq, lr], axis=0).astype(BF16),
                        rhs=jnp.concatenate([rbb, rkb], axis=0))

        st = [prologue(bi, p, ci) for (_, bi, p, ci) in group]
        amat = [_dot_nt(st[i]["lhs"], st[i]["rhs"]) for i in n]
        nab = [amat[i][0:rows, 0:rows] * cst_ref[0] for i in n]
        tinv = _tri_inverse(nab, cst_ref, c)
        nmk = [jnp.concatenate([amat[i][0:rows, rows:2 * rows] * cst_ref[0],
                                amat[i][rows:2 * rows, rows:2 * rows] * cst_ref[1]], axis=0).astype(BF16) for i in n]
        mrb = [(amat[i][rows:2 * rows, 0:rows] * cst_ref[1]).astype(BF16) for i in n]
        nm = [_dot(nmk[i], st[i]["vsb"]) for i in n]
        txb = [_dot(tinv[i].astype(BF16),
                    jnp.concatenate([nm[i][0:rows], st[i]["lq"]], axis=1).astype(BF16)).astype(BF16) for i in n]
        mx2 = [_dot(mrb[i], txb[i]) for i in n]
        tn1 = [_dot_tn(txb[i][:, LANES:2 * LANES], st[i]["rbb"]) for i in n]
        tn2 = [_dot_tn(jnp.concatenate([txb[i][:, 0:LANES], st[i]["vsb"]], axis=0), st[i]["rhs"]) for i in n]
        for i, (u, _, _, ci) in enumerate(group):
            y1_ref[u, ci] = nm[i][rows:2 * rows] + mx2[i][:, 0:LANES]
            rp_ref[u, ci] = (st[i]["lr"] - mx2[i][:, LANES:2 * LANES]).astype(BF16)
            mx_hi, mx_lo = _split2((eye_ref[...] - tn1[i]) * st[i]["g_last"])
            mxh_ref[u, ci] = mx_hi
            mxl_ref[u, ci] = mx_lo
            d1_ref[u, ci] = tn2[i] * st[i]["g_last"]

    def phase3(u, bi, p, ci):
        sl = rows_of(ci)
        ln = slice(p * LANES, (p + 1) * LANES)
        ld = lambda ref: ref[bi, sl, ln].astype(F32)
        yst = y1_ref[u, ci] + _dot_nt(rp_ref[u, ci], sh_ref[u, ci])
        hm = jnp.concatenate([jnp.broadcast_to(m0, (c, LANES)), jnp.broadcast_to(m1, (c, LANES))], axis=0)
        mean = jnp.sum(yst, axis=-1, keepdims=True) * (1.0 / hd)
        cen = (yst - mean) * hm
        var = jnp.sum(cen * cen, axis=-1, keepdims=True) * (1.0 / hd)
        yn = cen * lax.rsqrt(var + GN_EPS)
        rkv = stack(ld(r_ref) * ld(k_ref) * rk_ref[:, ln])
        bonus = jnp.sum(rkv, axis=-1, keepdims=True) * stack(ld(v_ref))
        yp = yn[0:c] + yn[c:rows]
        bp = bonus[0:c] + bonus[c:rows]
        o_ref[bi, sl, ln] = (yp * gng_ref[:, ln] + gnb_ref[:, ln] + bp).astype(o_ref.dtype)

    def chunk_loop(fn):
        def body(cg, carry):
            fn([(u, bi, p, cg * unroll + j) for j in range(unroll) for u, (bi, p) in enumerate(units)])
            return carry
        if nch == unroll:
            body(0, 0)
        else:
            lax.fori_loop(0, nch // unroll, body, 0)

    chunk_loop(phase1)

    def phase2(ci, ss):
        out = []
        for u in range(len(units)):
            s_hi, s_lo = _split2(ss[u])
            sh_ref[u, ci] = s_hi
            mh = mxh_ref[u, ci]
            out.append(_dot(s_hi, mh) + _dot(s_lo, mh) + _dot(s_hi, mxl_ref[u, ci]) + d1_ref[u, ci])
        return tuple(out)

    ss = tuple(s_ref[u] for u in range(len(units)))
    if nch == 1:
        ss = phase2(0, ss)
    else:
        ss = lax.fori_loop(0, nch, phase2, ss)
    for u in range(len(units)):
        s_ref[u] = ss[u]

    chunk_loop(lambda group: [phase3(*g) for g in group])

    @pl.when(ti == pl.num_programs(2) - 1)
    def _():
        for u, (bi, p) in enumerate(units):
            sf = s_ref[u]
            so_ref[bi, 2 * p] = sf[0:hd, 0:hd]
            so_ref[bi, 2 * p + 1] = sf[hd:2 * hd, hd:2 * hd]


def _wkv(r, lw, kn, b, v, k, s0, r_k, gn_g, gn_b, c, nbk, tb, npl, unroll):
    bsz, t, d = r.shape
    hd = s0.shape[-1]
    width = npl * LANES
    nch = tb // c
    assert nch % unroll == 0 and t % tb == 0 and d % width == 0
    rows = 2 * c
    nu = nbk * npl
    cst = _wkv_masks(c)
    eye = jnp.eye(LANES, dtype=F32)
    tok = pl.BlockSpec((nbk, tb, width), lambda i, j, tt: (i, tt, j))
    st = pl.BlockSpec((nbk, 2 * npl, hd, hd), lambda i, j, tt: (i, j, 0, 0))
    vec = pl.BlockSpec((1, width), lambda i, j, tt: (0, j))
    return pl.pallas_call(
        functools.partial(_wkv_kernel, c=c, hd=hd, unroll=unroll),
        out_shape=(jax.ShapeDtypeStruct((bsz, t, d), BF16),
                   jax.ShapeDtypeStruct(s0.shape, F32)),
        grid=(bsz // nbk, d // width, t // tb),
        in_specs=[tok, tok, tok, tok, tok, tok, st, vec, vec, vec,
                  pl.BlockSpec(cst.shape, lambda i, j, tt: (0, 0, 0)),
                  pl.BlockSpec(eye.shape, lambda i, j, tt: (0, 0))],
        out_specs=(tok, st),
        scratch_shapes=[pltpu.VMEM((nu, nch, rows, LANES), F32),
                        pltpu.VMEM((nu, nch, rows, LANES), BF16),
                        pltpu.VMEM((nu, nch, LANES, LANES), BF16),
                        pltpu.VMEM((nu, nch, LANES, LANES), BF16),
                        pltpu.VMEM((nu, nch, LANES, LANES), F32),
                        pltpu.VMEM((nu, nch, LANES, LANES), BF16),
                        pltpu.VMEM((nu, LANES, LANES), F32)],
        compiler_params=_cparams("parallel", "parallel", "arbitrary"),
        name="wkv_scan",
    )(r, lw, kn, b, v, k, s0, r_k, gn_g, gn_b, cst, eye)


def _mix_kernel(o_ref, g_ref, za_ref, gb_ref, x_ref, wout_ref, wo_ref, n2_ref, rwh_ref, rwl_ref, rb_ref,
                x1_ref, h_ref, route_ref, *, n_exp, n_grp):
    og = (o_ref[...].astype(F32) * g_ref[...].astype(F32)).astype(BF16)
    yb = _dot(og, wout_ref[...])
    mix = za_ref[...].astype(F32) + gb_ref[...].astype(F32) * yb
    x1 = x_ref[...] + _dot(mix.astype(BF16), wo_ref[...])
    x1_ref[...] = x1
    ms = jnp.mean(x1 * x1, axis=-1, keepdims=True)
    h = x1 * lax.rsqrt(ms + NORM_EPS) * n2_ref[...]
    h_ref[...] = h
    h_hi, h_lo = _split2(h)
    lg = _dot(h_hi, rwh_ref[...]) + _dot(h_lo, rwh_ref[...]) + _dot(h_hi, rwl_ref[...]) + rb_ref[...]
    lane = lax.broadcasted_iota(jnp.int32, lg.shape, 1)
    big = jnp.int32(1 << 20)
    neg = jnp.float32(-jnp.inf)
    gmask = (lane >= n_exp) & (lane < n_exp + n_grp)
    glog = jnp.where(gmask, lg, neg)
    gmax = jnp.max(glog, axis=-1, keepdims=True)
    gsel = jnp.min(jnp.where(glog == gmax, lane - n_exp, big), axis=-1, keepdims=True)
    pg = 1.0 / jnp.sum(jnp.where(gmask, jnp.exp(lg - gmax), 0.0), axis=-1, keepdims=True)
    epg = n_exp // n_grp
    emask = (lane < n_exp) & ((lane // epg) == gsel)
    el = jnp.where(emask, lg, neg)
    m1 = jnp.max(el, axis=-1, keepdims=True)
    i1 = jnp.min(jnp.where(el == m1, lane, big), axis=-1, keepdims=True)
    el2 = jnp.where(lane == i1, neg, el)
    m2 = jnp.max(el2, axis=-1, keepdims=True)
    i2 = jnp.min(jnp.where(el2 == m2, lane, big), axis=-1, keepdims=True)
    e2 = jnp.exp(m2 - m1)
    p1 = 1.0 / (1.0 + e2)
    p2 = e2 * p1
    route = jnp.where(lane == 0, i1.astype(F32),
                      jnp.where(lane == 1, i2.astype(F32),
                                jnp.where(lane == 2, pg * p1, jnp.where(lane == 3, pg * p2, 0.0))))
    route_ref[...] = route


def _mix(o2, g2, za2, gates2, x2, wout_b, wo_b, n2, rw_hi, rw_lo, rbias, n_exp, n_grp, tm):
    m, d = x2.shape
    tokb = pl.BlockSpec((tm, d), lambda i: (i, 0))
    full = lambda a: pl.BlockSpec(a.shape, lambda i: tuple(0 for _ in a.shape))
    return pl.pallas_call(
        functools.partial(_mix_kernel, n_exp=n_exp, n_grp=n_grp),
        out_shape=(jax.ShapeDtypeStruct((m, d), F32), jax.ShapeDtypeStruct((m, d), F32),
                   jax.ShapeDtypeStruct((m, LANES), F32)),
        grid=(m // tm,),
        in_specs=[tokb, tokb, tokb, pl.BlockSpec((tm, d), lambda i: (i, 1)), tokb,
                  full(wout_b), full(wo_b), full(n2), full(rw_hi), full(rw_lo), full(rbias)],
        out_specs=(tokb, tokb, pl.BlockSpec((tm, LANES), lambda i: (i, 0))),
        compiler_params=_cparams("parallel"),
        name="mix_router",
    )(o2, g2, za2, gates2, x2, wout_b, wo_b, n2, rw_hi, rw_lo, rbias)


def _expert_kernel(be_ref, tok_ref, nused_ref, h_ref, wg_ref, wu_ref, wd_ref, o_ref, buf_ref, sem_ref):
    i = pl.program_id(0)
    nused = nused_ref[0]

    def issue(blk, slot):
        def body(rr, carry):
            tok = tok_ref[blk * MOE_BLOCK + rr]
            pltpu.make_async_copy(h_ref.at[pl.ds(tok, 1)], buf_ref.at[slot, pl.ds(rr, 1)],
                                  sem_ref.at[slot]).start()
            return carry
        lax.fori_loop(0, MOE_BLOCK, body, 0)

    @pl.when(i == 0)
    def _():
        issue(0, 0)

    @pl.when(i + 1 < nused)
    def _():
        issue(i + 1, (i + 1) % 2)

    @pl.when(i < nused)
    def _():
        slot = i % 2
        pltpu.make_async_copy(h_ref.at[pl.ds(0, MOE_BLOCK)], buf_ref.at[slot], sem_ref.at[slot]).wait()
        x = buf_ref[slot].astype(BF16)
        hg = _dot(x, wg_ref[0])
        hu = _dot(x, wu_ref[0])
        act = (hg * _sigmoid(hg) * hu).astype(BF16)
        o_ref[...] = _dot(act, wd_ref[0])

    @pl.when(i >= nused)
    def _():
        o_ref[...] = jnp.zeros(o_ref.shape, o_ref.dtype)


def _experts(block_expert, row_tok, nused, h2, wg_b, wu_b, wd_b):
    m, d = h2.shape
    nb = block_expert.shape[0]
    de = wg_b.shape[-1]
    grid_spec = pltpu.PrefetchScalarGridSpec(
        num_scalar_prefetch=3,
        grid=(nb,),
        in_specs=[pl.BlockSpec(memory_space=pl.ANY),
                  pl.BlockSpec((1, d, de), lambda i, be, tk, nu: (be[i], 0, 0)),
                  pl.BlockSpec((1, d, de), lambda i, be, tk, nu: (be[i], 0, 0)),
                  pl.BlockSpec((1, de, d), lambda i, be, tk, nu: (be[i], 0, 0))],
        out_specs=pl.BlockSpec((MOE_BLOCK, d), lambda i, be, tk, nu: (i, 0)),
        scratch_shapes=[pltpu.VMEM((2, MOE_BLOCK, d), F32), pltpu.SemaphoreType.DMA((2,))],
    )
    return pl.pallas_call(
        _expert_kernel,
        out_shape=jax.ShapeDtypeStruct((nb * MOE_BLOCK, d), F32),
        grid_spec=grid_spec,
        compiler_params=_cparams("arbitrary"),
        name="expert_blocks",
    )(block_expert, row_tok, nused, h2, wg_b, wu_b, wd_b)


def _combine_kernel(dest_ref, yb_ref, x1_ref, route_ref, fg_ref, o_ref, buf_ref, sem_ref, *, tm):
    i = pl.program_id(0)
    n = pl.num_programs(0)

    def issue(blk, slot):
        def body(rr, carry):
            for j in range(2):
                dst = dest_ref[(blk * tm + rr) * 2 + j]
                pltpu.make_async_copy(yb_ref.at[pl.ds(dst, 1)], buf_ref.at[slot, j, pl.ds(rr, 1)],
                                      sem_ref.at[slot]).start()
            return carry
        lax.fori_loop(0, tm, body, 0)

    @pl.when(i == 0)
    def _():
        issue(0, 0)

    @pl.when(i + 1 < n)
    def _():
        issue(i + 1, (i + 1) % 2)

    slot = i % 2
    for j in range(2):
        pltpu.make_async_copy(yb_ref.at[pl.ds(0, tm)], buf_ref.at[slot, j], sem_ref.at[slot]).wait()
    route = route_ref[...]
    x2 = x1_ref[...] + (route[:, 2:3] * buf_ref[slot, 0] + route[:, 3:4] * buf_ref[slot, 1])
    ms = jnp.mean(x2 * x2, axis=-1, keepdims=True)
    o_ref[...] = x2 * lax.rsqrt(ms + NORM_EPS) * fg_ref[...]


def _combine(dest, yb, x1, route, fg, tm):
    m, d = x1.shape
    grid_spec = pltpu.PrefetchScalarGridSpec(
        num_scalar_prefetch=1,
        grid=(m // tm,),
        in_specs=[pl.BlockSpec(memory_space=pl.ANY),
                  pl.BlockSpec((tm, d), lambda i, ds: (i, 0)),
                  pl.BlockSpec((tm, LANES), lambda i, ds: (i, 0)),
                  pl.BlockSpec((1, d), lambda i, ds: (0, 0))],
        out_specs=pl.BlockSpec((tm, d), lambda i, ds: (i, 0)),
        scratch_shapes=[pltpu.VMEM((2, 2, tm, d), F32), pltpu.SemaphoreType.DMA((2,))],
    )
    return pl.pallas_call(
        functools.partial(_combine_kernel, tm=tm),
        out_shape=jax.ShapeDtypeStruct((m, d), F32),
        grid_spec=grid_spec,
        compiler_params=_cparams("arbitrary"),
        name="moe_combine",
    )(dest, yb, x1, route, fg)


def _dispatch_indices(route, n_exp):
    m = route.shape[0]
    e_flat = route[:, 0:2].astype(jnp.int32).reshape(2 * m)
    a = 2 * m
    nb = -(-a // MOE_BLOCK) + n_exp
    onehot = (e_flat[:, None] == jnp.arange(n_exp, dtype=jnp.int32)[None, :]).astype(jnp.int32)
    csum = jnp.cumsum(onehot, axis=0)
    rank = jnp.sum((csum - 1) * onehot, axis=1)
    counts = csum[-1]
    padded = ((counts + MOE_BLOCK - 1) // MOE_BLOCK) * MOE_BLOCK
    ends = jnp.cumsum(padded)
    starts = ends - padded
    dest = starts[e_flat] + rank
    tok_flat = jnp.repeat(jnp.arange(m, dtype=jnp.int32), 2)
    row_tok = jnp.zeros((nb * MOE_BLOCK,), jnp.int32).at[dest].set(tok_flat)
    block_expert = jnp.clip(jnp.searchsorted(ends, jnp.arange(nb, dtype=jnp.int32) * MOE_BLOCK, side='right'),
                            0, n_exp - 1).astype(jnp.int32)
    nused = (ends[-1] // MOE_BLOCK).astype(jnp.int32).reshape(1)
    return block_expert, row_tok, nused, dest.astype(jnp.int32)


def _pick_tile(t, pref):
    tile = min(t, pref)
    assert t % tile == 0, (t, tile)
    return tile


def _group_rows(nb, tt, target_rows):
    nbk = max(1, min(nb, target_rows // tt))
    while nb % nbk:
        nbk -= 1
    return nbk


def _layer(x3, conv_buf, shift_row, s0, w, final_g):
    bsz, t, d = x3.shape
    m = bsz * t
    cw, rw, gw = w["cw"], w["rw"], w["gw"]
    taps = w["dw"].shape[0]
    x2 = x3.reshape(m, d)
    tm = _pick_tile(m, 256)
    glu, prw, gates = _inproj(x2, w["n1"], w["w_in_b"], cw, rw, gw, tm)

    tt = _pick_tile(t, 256)
    nt = t // tt
    nbk = _group_rows(bsz * nt, tt, 256)
    glu4 = glu.reshape(bsz, nt, tt, cw)
    buf_pad = jnp.pad(conv_buf.astype(F32), ((0, 0), (CONV_HALO - (taps - 1), 0), (0, 0)))
    if nt > 1:
        assert tt >= CONV_HALO
        halo = jnp.concatenate([buf_pad[:, None], glu4[:, :-1, tt - CONV_HALO:, :]], axis=1)
    else:
        halo = buf_pad[:, None]
    za = _conv_branch(glu4.reshape(bsz * nt, tt, cw), halo.reshape(bsz * nt, CONV_HALO, cw),
                      gates.reshape(bsz * nt, tt, gw), w["dw"], w["dwb"], w["lng"], w["lnb"],
                      w["conv_out_b16"], w["conv_out_bias"], nbk)
    full_seq = jnp.concatenate([conv_buf.astype(F32), glu.reshape(bsz, t, cw)], axis=1)
    new_buf = full_seq[:, -(taps - 1):]

    prev_rw = _rowproj(shift_row.astype(F32), w["w_rw_b"])
    prw4 = prw.reshape(bsz, nt, tt, rw)
    if nt > 1:
        prev = jnp.concatenate([prev_rw[:, None], prw4[:, :-1, tt - 1, :].astype(F32)], axis=1)
    else:
        prev = prev_rw[:, None]
    r, lw, kn, b, v, k, g = _rwkv_prep(prw4.reshape(bsz * nt, tt, rw), prev.reshape(bsz * nt, 1, rw),
                                       w["mu"], w["w0"], w["w2p"], w["a0"], w["a2p"], w["g2b"],
                                       w["k_k"], w["k_a"], w["hsum"], w["hexp"], nbk)
    sh = lambda a: a.reshape(bsz, t, d)
    chunk = _pick_tile(t, WKV_CHUNK)
    tb = _pick_tile(t, WKV_TIME_BLOCK)
    nch = tb // chunk
    unroll = 2 if nch % 2 == 0 else 1
    wkv_nbk = _group_rows(bsz, 1, max(1, WKV_UNITS // (WKV_PAIRS * unroll)))
    o, s_new = _wkv(sh(r), sh(lw), sh(kn), sh(b), sh(v), sh(k), s0.astype(F32), w["r_k"], w["gn_g"],
                    w["gn_b"], chunk, wkv_nbk, tb, WKV_PAIRS, unroll)

    x1, h, route = _mix(o.reshape(m, d), g.reshape(m, d), za.reshape(m, d), gates, x2, w["rwkv_out_b16"],
                        w["w_o_b16"], w["n2"], w["rt_hi"], w["rt_lo"], w["rt_b"], w["n_exp"], w["n_grp"], tm)
    block_expert, row_tok, nused, dest = _dispatch_indices(route, w["n_exp"])
    yb = _experts(block_expert, row_tok, nused, h, w["wg_b"], w["wu_b"], w["wd_b"])
    y = _combine(dest, yb, x1, route, final_g, _pick_tile(m, 128))
    u_last = _rmsnorm_rows(x3[:, -1, :], w["n1"])
    return y.reshape(bsz, t, d), new_buf, u_last, s_new


def kernel(x_prompt, x_sample, state_conv, state_shift, state_wkv, norm1_g, w_in, conv_dw_w, conv_dw_b,
           conv_ln_g, conv_ln_b, conv_out_w, conv_out_b, rwkv_mu, rwkv_w0, rwkv_w2, rwkv_a0, rwkv_a2,
           rwkv_g2, rwkv_k_k, rwkv_k_a, rwkv_r_k, rwkv_gn_g, rwkv_gn_b, rwkv_out_w, w_o, norm2_g,
           router_group_w, router_group_b, router_expert_w, router_expert_b, expert_w_gate, expert_w_up,
           expert_w_down, final_norm_g):
    depth = norm1_g.shape[0]
    assert depth == 1, "single-layer trunk"
    l = 0
    d = x_prompt.shape[-1]
    cw = conv_out_w.shape[1]
    hd = state_wkv.shape[-1]
    heads = state_wkv.shape[2]
    assert heads * hd == d and 2 * hd == LANES
    dr, ir, gr = rwkv_w2.shape[1], rwkv_a2.shape[1], rwkv_g2.shape[1]
    assert dr + ir == LANES and gr == LANES
    rw = 3 * d + dr + ir + gr
    gw = 2 * d
    n_grp = router_group_w.shape[-1]
    n_exp = router_expert_w.shape[-1]
    row = lambda a: a.astype(F32).reshape(1, -1)

    head_of_lane = jnp.arange(d, dtype=jnp.int32) // hd
    hsum = (head_of_lane[:, None] == jnp.arange(LANES, dtype=jnp.int32)[None, :]).astype(BF16)
    rt_w = jnp.zeros((d, LANES), F32).at[:, :n_exp].set(router_expert_w[l]).at[:, n_exp:n_exp + n_grp].set(
        router_group_w[l])
    rt_hi = rt_w.astype(BF16)
    rt_lo = (rt_w - rt_hi.astype(F32)).astype(BF16)
    rt_b = jnp.zeros((1, LANES), F32).at[0, :n_exp].set(router_expert_b[l]).at[0, n_exp:n_exp + n_grp].set(
        router_group_b[l])
    w_in_b = w_in[l].astype(BF16)
    w = dict(
        cw=cw, rw=rw, gw=gw, n_exp=n_exp, n_grp=n_grp,
        n1=row(norm1_g[l]), w_in_b=w_in_b, w_rw_b=w_in_b[:, 2 * cw:2 * cw + rw],
        dw=conv_dw_w[l].astype(F32), dwb=row(conv_dw_b[l]), lng=row(conv_ln_g[l]), lnb=row(conv_ln_b[l]),
        conv_out_b16=conv_out_w[l].astype(BF16), conv_out_bias=row(conv_out_b[l]),
        mu=row(rwkv_mu[l]), w0=row(rwkv_w0[l]), a0=row(rwkv_a0[l]),
        w2p=jnp.concatenate([rwkv_w2[l], jnp.zeros((ir, d), F32)], axis=0).astype(BF16),
        a2p=jnp.concatenate([jnp.zeros((dr, d), F32), rwkv_a2[l]], axis=0).astype(BF16),
        g2b=rwkv_g2[l].astype(BF16), k_k=row(rwkv_k_k[l]), k_a=row(rwkv_k_a[l]),
        hsum=hsum, hexp=hsum.T, r_k=row(rwkv_r_k[l]), gn_g=row(rwkv_gn_g[l]), gn_b=row(rwkv_gn_b[l]),
        rwkv_out_b16=rwkv_out_w[l].astype(BF16), w_o_b16=w_o[l].astype(BF16), n2=row(norm2_g[l]),
        rt_hi=rt_hi, rt_lo=rt_lo, rt_b=rt_b,
        wg_b=expert_w_gate[l].astype(BF16), wu_b=expert_w_up[l].astype(BF16), wd_b=expert_w_down[l].astype(BF16),
    )
    fg = row(final_norm_g)
    bp = x_prompt.shape[0]
    taps = conv_dw_w.shape[1]
    yp, cbp, srp, sp = _layer(x_prompt.astype(F32), jnp.zeros((bp, taps - 1, cw), F32), jnp.zeros((bp, d), F32),
                              jnp.zeros((bp, heads, hd, hd), F32), w, fg)
    ys, cbs, srs, ss = _layer(x_sample.astype(F32), state_conv[l], state_shift[l], state_wkv[l], w, fg)
    sdt = state_wkv.dtype
    return (yp.astype(x_prompt.dtype), ys.astype(x_sample.dtype),
            cbp[None].astype(state_conv.dtype), srp[None].astype(state_shift.dtype), sp[None].astype(sdt),
            cbs[None].astype(state_conv.dtype), srs[None].astype(state_shift.dtype), ss[None].astype(sdt))
```

```python
import functools
import math

import jax
import jax.numpy as jnp
from jax import lax
from jax.experimental import pallas as pl
from jax.experimental.pallas import tpu as pltpu

F32 = jnp.float32
BF16 = jnp.bfloat16

NORM_EPS = 1e-6
LN_EPS = 1e-5
GN_EPS = 64e-5
LANES = 128
SUBLANES = 8
MOE_BLOCK = 128
CONV_HALO = 32
VMEM_LIMIT = 56 * 1024 * 1024
WKV_CHUNK = 64
WKV_TIME_BLOCK = 1024
WKV_PAIRS = 2
WKV_UNITS = 8
WKV_UNITS_SHORT = 32
DMA_UNROLL = 8


def _cparams(*sem):
    return pltpu.CompilerParams(dimension_semantics=sem, vmem_limit_bytes=VMEM_LIMIT)


def _dot(a, b):
    return jnp.dot(a, b, preferred_element_type=F32)


def _dot_nt(a, b):
    return lax.dot_general(a, b, (((1,), (1,)), ((), ())), preferred_element_type=F32)


def _dot_tn(a, b):
    return lax.dot_general(a, b, (((0,), (0,)), ((), ())), preferred_element_type=F32)


def _split2(x):
    hi = x.astype(BF16)
    lo = (x - hi.astype(F32)).astype(BF16)
    return hi, lo


def _sigmoid(x):
    return 1.0 / (1.0 + jnp.exp(-x))


def _col_chunks(n, width=512):
    out, c = [], 0
    while c < n:
        w = min(width, n - c)
        out.append((c, w))
        c += w
    return out


def _inproj_kernel(x_ref, g_ref, w_ref, glu_ref, prw_ref, gate_ref, *, cw, rw, gw):
    x = x_ref[...]
    ms = jnp.mean(x * x, axis=-1, keepdims=True)
    ub = (x * lax.rsqrt(ms + NORM_EPS) * g_ref[...]).astype(BF16)
    for c0, w in _col_chunks(cw):
        a = _dot(ub, w_ref[:, c0:c0 + w])
        gt = _dot(ub, w_ref[:, cw + c0:cw + c0 + w])
        glu_ref[:, c0:c0 + w] = a * _sigmoid(gt)
    for c0, w in _col_chunks(rw):
        prw_ref[:, c0:c0 + w] = _dot(ub, w_ref[:, 2 * cw + c0:2 * cw + c0 + w]).astype(BF16)
    for c0, w in _col_chunks(gw):
        s = 2 * cw + rw + c0
        gate_ref[:, c0:c0 + w] = _sigmoid(_dot(ub, w_ref[:, s:s + w])).astype(BF16)


def _inproj(x2, g1, w_in_b, cw, rw, gw, tm):
    m, d = x2.shape
    ncol = w_in_b.shape[1]
    return pl.pallas_call(
        functools.partial(_inproj_kernel, cw=cw, rw=rw, gw=gw),
        out_shape=(jax.ShapeDtypeStruct((m, cw), F32),
                   jax.ShapeDtypeStruct((m, rw), BF16),
                   jax.ShapeDtypeStruct((m, gw), BF16)),
        grid=(m // tm,),
        in_specs=[pl.BlockSpec((tm, d), lambda i: (i, 0)),
                  pl.BlockSpec((1, d), lambda i: (0, 0)),
                  pl.BlockSpec((d, ncol), lambda i: (0, 0), pipeline_mode=pl.Buffered(1))],
        out_specs=(pl.BlockSpec((tm, cw), lambda i: (i, 0)),
                   pl.BlockSpec((tm, rw), lambda i: (i, 0)),
                   pl.BlockSpec((tm, gw), lambda i: (i, 0))),
        compiler_params=_cparams("parallel"),
        name="inproj",
    )(x2, g1, w_in_b)


def _rowproj_kernel(x_ref, w_ref, o_ref):
    o_ref[...] = _dot(x_ref[...].astype(BF16), w_ref[...])


def _rowproj(x2, w_b):
    m, d = x2.shape
    n = w_b.shape[1]
    return pl.pallas_call(
        _rowproj_kernel,
        out_shape=jax.ShapeDtypeStruct((m, n), F32),
        grid=(1,),
        in_specs=[pl.BlockSpec((m, d), lambda i: (0, 0)), pl.BlockSpec((d, n), lambda i: (0, 0))],
        out_specs=pl.BlockSpec((m, n), lambda i: (0, 0)),
        compiler_params=_cparams("arbitrary"),
        name="shift_rowproj",
    )(x2, w_b)


def _rmsnorm_rows_kernel(x_ref, g_ref, o_ref):
    x = x_ref[...]
    ms = jnp.mean(x * x, axis=-1, keepdims=True)
    o_ref[...] = x * lax.rsqrt(ms + NORM_EPS) * g_ref[...]


def _rmsnorm_rows(x2, g1):
    m, d = x2.shape
    return pl.pallas_call(
        _rmsnorm_rows_kernel,
        out_shape=jax.ShapeDtypeStruct((m, d), F32),
        grid=(1,),
        in_specs=[pl.BlockSpec((m, d), lambda i: (0, 0)), pl.BlockSpec((1, d), lambda i: (0, 0))],
        out_specs=pl.BlockSpec((m, d), lambda i: (0, 0)),
        compiler_params=_cparams("arbitrary"),
        name="shift_rmsnorm",
    )(x2, g1)


def _conv_kernel(glu_ref, halo_ref, gate_ref, dw_ref, dwb_ref, lng_ref, lnb_ref, ow_ref, ob_ref,
                 o_ref, full_ref, shift_ref, *, taps):
    nbk, tt, cw = glu_ref.shape
    full_ref[:, 0:CONV_HALO, :] = halo_ref[...]
    full_ref[:, CONV_HALO:CONV_HALO + tt, :] = glu_ref[...]
    span = shift_ref.shape[2]
    for s in range(1, SUBLANES):
        shift_ref[s - 1] = full_ref[:, s:s + span, :]
    first = CONV_HALO - (taps - 1)
    acc = jnp.zeros((nbk, tt, cw), F32) + dwb_ref[...]
    for j in range(taps):
        s = (first + j) % SUBLANES
        q = first + j - s
        win = full_ref[:, q:q + tt, :] if s == 0 else shift_ref[s - 1, :, q:q + tt, :]
        acc = acc + win * dw_ref[j:j + 1, :]
    mean = jnp.mean(acc, axis=-1, keepdims=True)
    cen = acc - mean
    var = jnp.mean(cen * cen, axis=-1, keepdims=True)
    y = cen * lax.rsqrt(var + LN_EPS) * lng_ref[...] + lnb_ref[...]
    y = y * _sigmoid(y)
    y2 = y.reshape(nbk * tt, cw).astype(BF16)
    out = _dot(y2, ow_ref[...]) + ob_ref[...]
    gate = gate_ref[...].reshape(nbk * tt, -1).astype(F32)
    o_ref[...] = (out * gate).reshape(o_ref.shape).astype(o_ref.dtype)


def _conv_branch(glu3, halo3, gates3, dw, dwb, lng, lnb, ow_b, ob, nbk):
    nb, tt, cw = glu3.shape
    d = ow_b.shape[1]
    taps = dw.shape[0]
    full = lambda shape: pl.BlockSpec(shape, lambda i: tuple(0 for _ in shape))
    return pl.pallas_call(
        functools.partial(_conv_kernel, taps=taps),
        out_shape=jax.ShapeDtypeStruct((nb, tt, d), BF16),
        grid=(nb // nbk,),
        in_specs=[pl.BlockSpec((nbk, tt, cw), lambda i: (i, 0, 0)),
                  pl.BlockSpec((nbk, CONV_HALO, cw), lambda i: (i, 0, 0)),
                  pl.BlockSpec((nbk, tt, d), lambda i: (i, 0, 0)),
                  full(dw.shape), full(dwb.shape), full(lng.shape), full(lnb.shape),
                  full(ow_b.shape), full(ob.shape)],
        out_specs=pl.BlockSpec((nbk, tt, d), lambda i: (i, 0, 0)),
        scratch_shapes=[pltpu.VMEM((nbk, CONV_HALO + tt, cw), F32),
                        pltpu.VMEM((SUBLANES - 1, nbk, CONV_HALO - SUBLANES + tt, cw), F32)],
        compiler_params=_cparams("parallel"),
        name="conv_branch",
    )(glu3, halo3, gates3, dw, dwb, lng, lnb, ow_b, ob)


def _prep_kernel(p_ref, prev_ref, mu_ref, w0_ref, w2_ref, a0_ref, a2_ref, g2_ref, kk_ref, ka_ref,
                 hsum_ref, hexp_ref,
                 r_ref, lw_ref, kn_ref, b_ref, v_ref, k_ref, g_ref, sh_ref, *, d):
    nbk, tt, rw = p_ref.shape
    sh_ref[:, 8:8 + tt, :] = p_ref[...].astype(F32)
    sh_ref[:, 7:8, :] = prev_ref[...]
    p = sh_ref[:, 8:8 + tt, :]
    pprev = sh_ref[:, 7:7 + tt, :]
    xm = (p + mu_ref[...] * (pprev - p)).reshape(nbk * tt, rw)
    r = xm[:, 0:d]
    k = xm[:, d:2 * d]
    v = xm[:, 2 * d:3 * d]
    wa = xm[:, 3 * d:3 * d + LANES]
    gd = xm[:, 3 * d + LANES:3 * d + 2 * LANES]
    lw = _dot(jnp.tanh(wa).astype(BF16), w2_ref[...])
    la = _dot(wa.astype(BF16), a2_ref[...])
    logw = -math.exp(-0.5) * _sigmoid(w0_ref[...] + lw)
    a = _sigmoid(a0_ref[...] + la)
    g = _dot(_sigmoid(gd).astype(BF16), g2_ref[...])
    kk = k * kk_ref[...]
    sq_hi, sq_lo = _split2(kk * kk)
    ss = _dot(sq_hi, hsum_ref[...]) + _dot(sq_lo, hsum_ref[...])
    inv = 1.0 / jnp.maximum(jnp.sqrt(ss), 1e-12)
    inv_hi, inv_lo = _split2(inv)
    kn = kk * (_dot(inv_hi, hexp_ref[...]) + _dot(inv_lo, hexp_ref[...]))
    put = lambda ref, val: ref.__setitem__(Ellipsis, val.reshape(ref.shape).astype(ref.dtype))
    put(r_ref, r)
    put(lw_ref, logw)
    put(kn_ref, kn)
    put(b_ref, kn * a)
    put(v_ref, v)
    put(k_ref, k * (1.0 + (a - 1.0) * ka_ref[...]))
    put(g_ref, g)


def _rwkv_prep(prw3, prev3, mu, w0, w2p, a0, a2p, g2b, k_k, k_a, hsum, hexp, nbk):
    nb, tt, rw = prw3.shape
    d = w0.shape[1]
    full = lambda a: pl.BlockSpec(a.shape, lambda i: tuple(0 for _ in a.shape))
    tok = pl.BlockSpec((nbk, tt, d), lambda i: (i, 0, 0))
    o32 = jax.ShapeDtypeStruct((nb, tt, d), F32)
    o16 = jax.ShapeDtypeStruct((nb, tt, d), BF16)
    return pl.pallas_call(
        functools.partial(_prep_kernel, d=d),
        out_shape=(o16, o32, o16, o16, o16, o16, o16),
        grid=(nb // nbk,),
        in_specs=[pl.BlockSpec((nbk, tt, rw), lambda i: (i, 0, 0)),
                  pl.BlockSpec((nbk, 1, rw), lambda i: (i, 0, 0)),
                  full(mu), full(w0), full(w2p), full(a0), full(a2p), full(g2b), full(k_k), full(k_a),
                  full(hsum), full(hexp)],
        out_specs=(tok,) * 7,
        scratch_shapes=[pltpu.VMEM((nbk, 8 + tt, rw), F32)],
        compiler_params=_cparams("parallel"),
        name="rwkv_prep",
    )(prw3, prev3, mu, w0, w2p, a0, a2p, g2b, k_k, k_a, hsum, hexp)


def _wkv_base(c):
    return min(c, 16)


def _wkv_masks(c):
    rows = 2 * c
    ri = jnp.arange(rows, dtype=jnp.int32)[:, None]
    ci = jnp.arange(rows, dtype=jnp.int32)[None, :]
    same = (ri // c) == (ci // c)
    blk = lambda s: (ri // s) == (ci // s)
    out = [-(same & ((ri % c) > (ci % c))).astype(F32), (same & ((ri % c) >= (ci % c))).astype(F32),
           (ri == ci).astype(F32), blk(_wkv_base(c)).astype(F32)]
    s = _wkv_base(c)
    while s < c:
        out.append((blk(2 * s) & jnp.logical_not(blk(s))).astype(F32))
        s *= 2
    return jnp.stack(out)


def _tri_inverse(nabs, cst_ref, c):
    base = _wkv_base(c)
    n = range(len(nabs))
    nds = [nabs[i] * cst_ref[3] for i in n]
    ps = [cst_ref[2] + nds[i] for i in n]
    xbs = [nds[i].astype(BF16) for i in n]
    for _ in range(int(math.log2(base)) - 1):
        xbs = [_dot(xbs[i], xbs[i]).astype(BF16) for i in n]
        ps = [ps[i] + _dot(ps[i].astype(BF16), xbs[i]) for i in n]
    s, lvl = base, 4
    while s < c:
        pbs = [ps[i].astype(BF16) for i in n]
        ts = [_dot(pbs[i], (nabs[i] * cst_ref[lvl]).astype(BF16)).astype(BF16) for i in n]
        ps = [ps[i] + _dot(ts[i], pbs[i]) for i in n]
        s *= 2
        lvl += 1
    return ps


def _wkv_kernel(r_ref, lw_ref, kn_ref, b_ref, v_ref, k_ref, s0_ref, rk_ref, gng_ref, gnb_ref, cst_ref, eye_ref,
                o_ref, so_ref, y1_ref, rp_ref, mxh_ref, mxl_ref, d1_ref, sh_ref, s_ref, *, c, hd, unroll):
    nbk, tb, width = r_ref.shape
    npl = width // LANES
    nch = tb // c
    rows = 2 * c
    ti = pl.program_id(2)
    lane = lax.broadcasted_iota(jnp.int32, (1, LANES), 1)
    m0 = (lane < hd).astype(F32)
    m1 = 1.0 - m0
    tpos = lax.broadcasted_iota(jnp.int32, (c, LANES), 0)
    units = [(bi, p) for bi in range(nbk) for p in range(npl)]

    def stack(x):
        return jnp.concatenate([x * m0, x * m1], axis=0)

    def rows_of(ci):
        if isinstance(ci, int):
            return pl.ds(ci * c, c)
        return pl.ds(pl.multiple_of(ci * c, c), c)

    @pl.when(ti == 0)
    def _():
        z = jnp.zeros((hd, hd), F32)
        for u, (bi, p) in enumerate(units):
            s_ref[u] = jnp.concatenate(
                [jnp.concatenate([s0_ref[bi, 2 * p], z], axis=1),
                 jnp.concatenate([z, s0_ref[bi, 2 * p + 1]], axis=1)], axis=0)

    def phase1(group):
        n = range(len(group))

        def prologue(bi, p, ci):
            sl = rows_of(ci)
            ln = slice(p * LANES, (p + 1) * LANES)
            ld = lambda ref: ref[bi, sl, ln].astype(F32)
            lw = lw_ref[bi, sl, ln]
            cum = lw
            s = 1
            while s < c:
                cum = cum + jnp.where(tpos >= s, pltpu.roll(cum, s, axis=0), 0.0)
                s *= 2
            g_in = jnp.exp(cum)
            g_inv = jnp.exp(-cum)
            g_prev = jnp.exp(cum - lw)
            lq = stack(ld(kn_ref) * g_prev)
            lr = stack(ld(r_ref) * g_in)
            rbb = stack(ld(b_ref) * g_inv).astype(BF16)
            rkb = stack(ld(k_ref) * g_inv).astype(BF16)
            vsb = stack(ld(v_ref)).astype(BF16)
            return dict(g_last=g_in[c - 1:c, :], lq=lq, lr=lr, rbb=rbb, vsb=vsb,
                        lhs=jnp.concatenate([lq, lr], axis=0).astype(BF16),
                        rhs=jnp.concatenate([rbb, rkb], axis=0))

        st = [prologue(bi, p, ci) for (_, bi, p, ci) in group]
        amat = [_dot_nt(st[i]["lhs"], st[i]["rhs"]) for i in n]
        nab = [amat[i][0:rows, 0:rows] * cst_ref[0] for i in n]
        tinv = _tri_inverse(nab, cst_ref, c)
        nmk = [jnp.concatenate([amat[i][0:rows, rows:2 * rows] * cst_ref[0],
                                amat[i][rows:2 * rows, rows:2 * rows] * cst_ref[1]], axis=0).astype(BF16) for i in n]
        mrb = [(amat[i][rows:2 * rows, 0:rows] * cst_ref[1]).astype(BF16) for i in n]
        nm = [_dot(nmk[i], st[i]["vsb"]) for i in n]
        txb = [_dot(tinv[i].astype(BF16),
                    jnp.concatenate([nm[i][0:rows], st[i]["lq"]], axis=1).astype(BF16)).astype(BF16) for i in n]
        mx2 = [_dot(mrb[i], txb[i]) for i in n]
        tn1 = [_dot_tn(txb[i][:, LANES:2 * LANES], st[i]["rbb"]) for i in n]
        tn2 = [_dot_tn(jnp.concatenate([txb[i][:, 0:LANES], st[i]["vsb"]], axis=0), st[i]["rhs"]) for i in n]
        for i, (u, _, _, ci) in enumerate(group):
            y1_ref[u, ci] = nm[i][rows:2 * rows] + mx2[i][:, 0:LANES]
            rp_ref[u, ci] = (st[i]["lr"] - mx2[i][:, LANES:2 * LANES]).astype(BF16)
            mx_hi, mx_lo = _split2((eye_ref[...] - tn1[i]) * st[i]["g_last"])
            mxh_ref[u, ci] = mx_hi
            mxl_ref[u, ci] = mx_lo
            d1_ref[u, ci] = tn2[i] * st[i]["g_last"]

    def phase3(u, bi, p, ci):
        sl = rows_of(ci)
        ln = slice(p * LANES, (p + 1) * LANES)
        ld = lambda ref: ref[bi, sl, ln].astype(F32)
        yst = y1_ref[u, ci] + _dot_nt(rp_ref[u, ci], sh_ref[u, ci])
        hm = jnp.concatenate([jnp.broadcast_to(m0, (c, LANES)), jnp.broadcast_to(m1, (c, LANES))], axis=0)
        mean = jnp.sum(yst, axis=-1, keepdims=True) * (1.0 / hd)
        cen = (yst - mean) * hm
        var = jnp.sum(cen * cen, axis=-1, keepdims=True) * (1.0 / hd)
        yn = cen * lax.rsqrt(var + GN_EPS)
        rkv = stack(ld(r_ref) * ld(k_ref) * rk_ref[:, ln])
        bonus = jnp.sum(rkv, axis=-1, keepdims=True) * stack(ld(v_ref))
        yp = yn[0:c] + yn[c:rows]
        bp = bonus[0:c] + bonus[c:rows]
        o_ref[bi, sl, ln] = (yp * gng_ref[:, ln] + gnb_ref[:, ln] + bp).astype(o_ref.dtype)

    def chunk_loop(fn):
        def body(cg, carry):
            fn([(u, bi, p, cg * unroll + j) for j in range(unroll) for u, (bi, p) in enumerate(units)])
            return carry
        if nch == unroll:
            body(0, 0)
        else:
            lax.fori_loop(0, nch // unroll, body, 0)

    chunk_loop(phase1)

    def phase2(ci, ss):
        out = []
        for u in range(len(units)):
            s_hi, s_lo = _split2(ss[u])
            sh_ref[u, ci] = s_hi
            mh = mxh_ref[u, ci]
            out.append(_dot(s_hi, mh) + _dot(s_lo, mh) + _dot(s_hi, mxl_ref[u, ci]) + d1_ref[u, ci])
        return tuple(out)

    ss = tuple(s_ref[u] for u in range(len(units)))
    if nch == 1:
        ss = phase2(0, ss)
    else:
        ss = lax.fori_loop(0, nch, phase2, ss)
    for u in range(len(units)):
        s_ref[u] = ss[u]

    chunk_loop(lambda group: [phase3(*g) for g in group])

    @pl.when(ti == pl.num_programs(2) - 1)
    def _():
        for u, (bi, p) in enumerate(units):
            sf = s_ref[u]
            so_ref[bi, 2 * p] = sf[0:hd, 0:hd]
            so_ref[bi, 2 * p + 1] = sf[hd:2 * hd, hd:2 * hd]


def _wkv(r, lw, kn, b, v, k, s0, r_k, gn_g, gn_b, c, nbk, tb, npl, unroll):
    bsz, t, d = r.shape
    hd = s0.shape[-1]
    width = npl * LANES
    nch = tb // c
    assert nch % unroll == 0 and t % tb == 0 and d % width == 0
    rows = 2 * c
    nu = nbk * npl
    cst = _wkv_masks(c)
    eye = jnp.eye(LANES, dtype=F32)
    tok = pl.BlockSpec((nbk, tb, width), lambda i, j, tt: (i, tt, j))
    st = pl.BlockSpec((nbk, 2 * npl, hd, hd), lambda i, j, tt: (i, j, 0, 0))
    vec = pl.BlockSpec((1, width), lambda i, j, tt: (0, j))
    return pl.pallas_call(
        functools.partial(_wkv_kernel, c=c, hd=hd, unroll=unroll),
        out_shape=(jax.ShapeDtypeStruct((bsz, t, d), BF16),
                   jax.ShapeDtypeStruct(s0.shape, F32)),
        grid=(bsz // nbk, d // width, t // tb),
        in_specs=[tok, tok, tok, tok, tok, tok, st, vec, vec, vec,
                  pl.BlockSpec(cst.shape, lambda i, j, tt: (0, 0, 0)),
                  pl.BlockSpec(eye.shape, lambda i, j, tt: (0, 0))],
        out_specs=(tok, st),
        scratch_shapes=[pltpu.VMEM((nu, nch, rows, LANES), F32),
                        pltpu.VMEM((nu, nch, rows, LANES), BF16),
                        pltpu.VMEM((nu, nch, LANES, LANES), BF16),
                        pltpu.VMEM((nu, nch, LANES, LANES), BF16),
                        pltpu.VMEM((nu, nch, LANES, LANES), F32),
                        pltpu.VMEM((nu, nch, LANES, LANES), BF16),
                        pltpu.VMEM((nu, LANES, LANES), F32)],
        compiler_params=_cparams("parallel", "parallel", "arbitrary"),
        name="wkv_scan",
    )(r, lw, kn, b, v, k, s0, r_k, gn_g, gn_b, cst, eye)


def _mix_kernel(o_ref, g_ref, za_ref, gb_ref, x_ref, wout_ref, wo_ref, n2_ref, rwh_ref, rwl_ref, rb_ref,
                x1_ref, h_ref, route_ref, *, n_exp, n_grp):
    og = (o_ref[...].astype(F32) * g_ref[...].astype(F32)).astype(BF16)
    yb = _dot(og, wout_ref[...])
    mix = za_ref[...].astype(F32) + gb_ref[...].astype(F32) * yb
    x1 = x_ref[...] + _dot(mix.astype(BF16), wo_ref[...])
    x1_ref[...] = x1
    ms = jnp.mean(x1 * x1, axis=-1, keepdims=True)
    h = x1 * lax.rsqrt(ms + NORM_EPS) * n2_ref[...]
    h_ref[...] = h
    h_hi, h_lo = _split2(h)
    lg = _dot(h_hi, rwh_ref[...]) + _dot(h_lo, rwh_ref[...]) + _dot(h_hi, rwl_ref[...]) + rb_ref[...]
    lane = lax.broadcasted_iota(jnp.int32, lg.shape, 1)
    big = jnp.int32(1 << 20)
    neg = jnp.float32(-jnp.inf)
    gmask = (lane >= n_exp) & (lane < n_exp + n_grp)
    glog = jnp.where(gmask, lg, neg)
    gmax = jnp.max(glog, axis=-1, keepdims=True)
    gsel = jnp.min(jnp.where(glog == gmax, lane - n_exp, big), axis=-1, keepdims=True)
    pg = 1.0 / jnp.sum(jnp.where(gmask, jnp.exp(lg - gmax), 0.0), axis=-1, keepdims=True)
    epg = n_exp // n_grp
    emask = (lane < n_exp) & ((lane // epg) == gsel)
    el = jnp.where(emask, lg, neg)
    m1 = jnp.max(el, axis=-1, keepdims=True)
    i1 = jnp.min(jnp.where(el == m1, lane, big), axis=-1, keepdims=True)
    el2 = jnp.where(lane == i1, neg, el)
    m2 = jnp.max(el2, axis=-1, keepdims=True)
    i2 = jnp.min(jnp.where(el2 == m2, lane, big), axis=-1, keepdims=True)
    e2 = jnp.exp(m2 - m1)
    p1 = 1.0 / (1.0 + e2)
    p2 = e2 * p1
    route = jnp.where(lane == 0, i1.astype(F32),
                      jnp.where(lane == 1, i2.astype(F32),
                                jnp.where(lane == 2, pg * p1, jnp.where(lane == 3, pg * p2, 0.0))))
    route_ref[...] = route


def _mix(o2, g2, za2, gates2, x2, wout_b, wo_b, n2, rw_hi, rw_lo, rbias, n_exp, n_grp, tm):
    m, d = x2.shape
    tokb = pl.BlockSpec((tm, d), lambda i: (i, 0))
    full = lambda a: pl.BlockSpec(a.shape, lambda i: tuple(0 for _ in a.shape))
    return pl.pallas_call(
        functools.partial(_mix_kernel, n_exp=n_exp, n_grp=n_grp),
        out_shape=(jax.ShapeDtypeStruct((m, d), F32), jax.ShapeDtypeStruct((m, d), F32),
                   jax.ShapeDtypeStruct((m, LANES), F32)),
        grid=(m // tm,),
        in_specs=[tokb, tokb, tokb, pl.BlockSpec((tm, d), lambda i: (i, 1)), tokb,
                  full(wout_b), full(wo_b), full(n2), full(rw_hi), full(rw_lo), full(rbias)],
        out_specs=(tokb, tokb, pl.BlockSpec((tm, LANES), lambda i: (i, 0))),
        compiler_params=_cparams("parallel"),
        name="mix_router",
    )(o2, g2, za2, gates2, x2, wout_b, wo_b, n2, rw_hi, rw_lo, rbias)


def _dispatch_kernel(dest_ref, h_ref, xinit_ref, xb_ref, sem_ref, *, tq):
    del xinit_ref
    i = pl.program_id(0)

    def body(rr, carry):
        t = i * tq + rr
        for j in range(2):
            pltpu.make_async_copy(h_ref.at[pl.ds(t, 1)], xb_ref.at[pl.ds(dest_ref[2 * t + j], 1)],
                                  sem_ref.at[0]).start()
        return carry

    lax.fori_loop(0, tq, body, 0, unroll=DMA_UNROLL)
    pltpu.make_async_copy(h_ref.at[pl.ds(0, 2 * tq)], xb_ref.at[pl.ds(0, 2 * tq)], sem_ref.at[0]).wait()


def _dispatch_rows(dest, h2, n_rows, tq):
    m, d = h2.shape
    assert m % tq == 0 and 2 * tq <= m
    xinit = jnp.zeros((n_rows, d), h2.dtype)
    grid_spec = pltpu.PrefetchScalarGridSpec(
        num_scalar_prefetch=1,
        grid=(m // tq,),
        in_specs=[pl.BlockSpec(memory_space=pl.ANY), pl.BlockSpec(memory_space=pl.ANY)],
        out_specs=pl.BlockSpec(memory_space=pl.ANY),
        scratch_shapes=[pltpu.SemaphoreType.DMA((1,))],
    )
    return pl.pallas_call(
        functools.partial(_dispatch_kernel, tq=tq),
        out_shape=jax.ShapeDtypeStruct((n_rows, d), h2.dtype),
        grid_spec=grid_spec,
        input_output_aliases={2: 0},
        compiler_params=_cparams("arbitrary"),
        name="moe_dispatch",
    )(dest, h2, xinit)


def _expert_kernel(be_ref, nused_ref, x_ref, wg_ref, wu_ref, wd_ref, o_ref):
    i = pl.program_id(0)

    @pl.when(i < nused_ref[0])
    def _():
        x = x_ref[...].astype(BF16)
        hg = _dot(x, wg_ref[0])
        hu = _dot(x, wu_ref[0])
        act = (hg * _sigmoid(hg) * hu).astype(BF16)
        o_ref[...] = _dot(act, wd_ref[0])

    @pl.when(i >= nused_ref[0])
    def _():
        o_ref[...] = jnp.zeros(o_ref.shape, o_ref.dtype)


def _experts(block_expert, nused, xb, wg_b, wu_b, wd_b):
    p, d = xb.shape
    nb = block_expert.shape[0]
    de = wg_b.shape[-1]
    grid_spec = pltpu.PrefetchScalarGridSpec(
        num_scalar_prefetch=2,
        grid=(nb,),
        in_specs=[pl.BlockSpec((MOE_BLOCK, d), lambda i, be, nu: (i, 0)),
                  pl.BlockSpec((1, d, de), lambda i, be, nu: (be[i], 0, 0)),
                  pl.BlockSpec((1, d, de), lambda i, be, nu: (be[i], 0, 0)),
                  pl.BlockSpec((1, de, d), lambda i, be, nu: (be[i], 0, 0))],
        out_specs=pl.BlockSpec((MOE_BLOCK, d), lambda i, be, nu: (i, 0)),
    )
    return pl.pallas_call(
        _expert_kernel,
        out_shape=jax.ShapeDtypeStruct((p, d), F32),
        grid_spec=grid_spec,
        compiler_params=_cparams("arbitrary"),
        name="expert_blocks",
    )(block_expert, nused, xb, wg_b, wu_b, wd_b)


def _combine_kernel(dest_ref, yb_ref, x1_ref, route_ref, fg_ref, o_ref, buf_ref, sem_ref, *, tm):
    i = pl.program_id(0)
    n = pl.num_programs(0)

    def issue(blk, slot):
        def body(rr, carry):
            for j in range(2):
                dst = dest_ref[(blk * tm + rr) * 2 + j]
                pltpu.make_async_copy(yb_ref.at[pl.ds(dst, 1)], buf_ref.at[slot, j, pl.ds(rr, 1)],
                                      sem_ref.at[slot]).start()
            return carry
        lax.fori_loop(0, tm, body, 0, unroll=DMA_UNROLL)

    @pl.when(i == 0)
    def _():
        issue(0, 0)

    @pl.when(i + 1 < n)
    def _():
        issue(i + 1, (i + 1) % 2)

    slot = i % 2
    for j in range(2):
        pltpu.make_async_copy(yb_ref.at[pl.ds(0, tm)], buf_ref.at[slot, j], sem_ref.at[slot]).wait()
    route = route_ref[...]
    x2 = x1_ref[...] + (route[:, 2:3] * buf_ref[slot, 0] + route[:, 3:4] * buf_ref[slot, 1])
    ms = jnp.mean(x2 * x2, axis=-1, keepdims=True)
    o_ref[...] = x2 * lax.rsqrt(ms + NORM_EPS) * fg_ref[...]


def _combine(dest, yb, x1, route, fg, tm):
    m, d = x1.shape
    grid_spec = pltpu.PrefetchScalarGridSpec(
        num_scalar_prefetch=1,
        grid=(m // tm,),
        in_specs=[pl.BlockSpec(memory_space=pl.ANY),
                  pl.BlockSpec((tm, d), lambda i, ds: (i, 0)),
                  pl.BlockSpec((tm, LANES), lambda i, ds: (i, 0)),
                  pl.BlockSpec((1, d), lambda i, ds: (0, 0))],
        out_specs=pl.BlockSpec((tm, d), lambda i, ds: (i, 0)),
        scratch_shapes=[pltpu.VMEM((2, 2, tm, d), F32), pltpu.SemaphoreType.DMA((2,))],
    )
    return pl.pallas_call(
        functools.partial(_combine_kernel, tm=tm),
        out_shape=jax.ShapeDtypeStruct((m, d), F32),
        grid_spec=grid_spec,
        compiler_params=_cparams("arbitrary"),
        name="moe_combine",
    )(dest, yb, x1, route, fg)


def _dispatch_indices(route, n_exp):
    m = route.shape[0]
    e_flat = route[:, 0:2].astype(jnp.int32).reshape(2 * m)
    a = 2 * m
    nb = -(-a // MOE_BLOCK) + n_exp
    onehot = (e_flat[:, None] == jnp.arange(n_exp, dtype=jnp.int32)[None, :]).astype(jnp.int32)
    csum = jnp.cumsum(onehot, axis=0)
    counts = csum[-1]
    padded = ((counts + MOE_BLOCK - 1) // MOE_BLOCK) * MOE_BLOCK
    ends = jnp.cumsum(padded)
    starts = ends - padded
    dest = jnp.sum((csum - 1 + starts[None, :]) * onehot, axis=1)
    blk_start = jnp.arange(nb, dtype=jnp.int32) * MOE_BLOCK
    block_expert = jnp.minimum(jnp.sum((ends[None, :] <= blk_start[:, None]).astype(jnp.int32), axis=1),
                               n_exp - 1)
    nused = (ends[-1] // MOE_BLOCK).astype(jnp.int32).reshape(1)
    return block_expert, nused, dest.astype(jnp.int32), nb * MOE_BLOCK


def _pick_tile(t, pref):
    tile = min(t, pref)
    assert t % tile == 0, (t, tile)
    return tile


def _group_rows(nb, tt, target_rows):
    nbk = max(1, min(nb, target_rows // tt))
    while nb % nbk:
        nbk -= 1
    return nbk


def _layer(x3, conv_buf, shift_row, s0, w, final_g):
    bsz, t, d = x3.shape
    m = bsz * t
    cw, rw, gw = w["cw"], w["rw"], w["gw"]
    taps = w["dw"].shape[0]
    x2 = x3.reshape(m, d)
    tm = _pick_tile(m, 256)
    glu, prw, gates = _inproj(x2, w["n1"], w["w_in_b"], cw, rw, gw, tm)

    tt = _pick_tile(t, 256)
    nt = t // tt
    nbk = _group_rows(bsz * nt, tt, 256)
    glu4 = glu.reshape(bsz, nt, tt, cw)
    buf_pad = jnp.pad(conv_buf.astype(F32), ((0, 0), (CONV_HALO - (taps - 1), 0), (0, 0)))
    if nt > 1:
        assert tt >= CONV_HALO
        halo = jnp.concatenate([buf_pad[:, None], glu4[:, :-1, tt - CONV_HALO:, :]], axis=1)
    else:
        halo = buf_pad[:, None]
    za = _conv_branch(glu4.reshape(bsz * nt, tt, cw), halo.reshape(bsz * nt, CONV_HALO, cw),
                      gates.reshape(bsz * nt, tt, gw), w["dw"], w["dwb"], w["lng"], w["lnb"],
                      w["conv_out_b16"], w["conv_out_bias"], nbk)
    full_seq = jnp.concatenate([conv_buf.astype(F32), glu.reshape(bsz, t, cw)], axis=1)
    new_buf = full_seq[:, -(taps - 1):]

    prev_rw = _rowproj(shift_row.astype(F32), w["w_rw_b"])
    prw4 = prw.reshape(bsz, nt, tt, rw)
    if nt > 1:
        prev = jnp.concatenate([prev_rw[:, None], prw4[:, :-1, tt - 1, :].astype(F32)], axis=1)
    else:
        prev = prev_rw[:, None]
    r, lw, kn, b, v, k, g = _rwkv_prep(prw4.reshape(bsz * nt, tt, rw), prev.reshape(bsz * nt, 1, rw),
                                       w["mu"], w["w0"], w["w2p"], w["a0"], w["a2p"], w["g2b"],
                                       w["k_k"], w["k_a"], w["hsum"], w["hexp"], nbk)
    sh = lambda a: a.reshape(bsz, t, d)
    chunk = _pick_tile(t, WKV_CHUNK)
    tb = _pick_tile(t, WKV_TIME_BLOCK)
    nch = tb // chunk
    unroll = 2 if nch % 2 == 0 else 1
    n_units = WKV_UNITS_SHORT if (nch == 1 and chunk < 16) else WKV_UNITS
    wkv_nbk = _group_rows(bsz, 1, max(1, n_units // (WKV_PAIRS * unroll)))
    o, s_new = _wkv(sh(r), sh(lw), sh(kn), sh(b), sh(v), sh(k), s0.astype(F32), w["r_k"], w["gn_g"],
                    w["gn_b"], chunk, wkv_nbk, tb, WKV_PAIRS, unroll)

    x1, h, route = _mix(o.reshape(m, d), g.reshape(m, d), za.reshape(m, d), gates, x2, w["rwkv_out_b16"],
                        w["w_o_b16"], w["n2"], w["rt_hi"], w["rt_lo"], w["rt_b"], w["n_exp"], w["n_grp"], tm)
    block_expert, nused, dest, n_rows = _dispatch_indices(route, w["n_exp"])
    xb = _dispatch_rows(dest, h, n_rows, _pick_tile(m // 2, 512))
    yb = _experts(block_expert, nused, xb, w["wg_b"], w["wu_b"], w["wd_b"])
    y = _combine(dest, yb, x1, route, final_g, _pick_tile(m, 128))
    u_last = _rmsnorm_rows(x3[:, -1, :], w["n1"])
    return y.reshape(bsz, t, d), new_buf, u_last, s_new


def kernel(x_prompt, x_sample, state_conv, state_shift, state_wkv, norm1_g, w_in, conv_dw_w, conv_dw_b,
           conv_ln_g, conv_ln_b, conv_out_w, conv_out_b, rwkv_mu, rwkv_w0, rwkv_w2, rwkv_a0, rwkv_a2,
           rwkv_g2, rwkv_k_k, rwkv_k_a, rwkv_r_k, rwkv_gn_g, rwkv_gn_b, rwkv_out_w, w_o, norm2_g,
           router_group_w, router_group_b, router_expert_w, router_expert_b, expert_w_gate, expert_w_up,
           expert_w_down, final_norm_g):
    depth = norm1_g.shape[0]
    assert depth == 1, "single-layer trunk"
    l = 0
    d = x_prompt.shape[-1]
    cw = conv_out_w.shape[1]
    hd = state_wkv.shape[-1]
    heads = state_wkv.shape[2]
    assert heads * hd == d and 2 * hd == LANES
    dr, ir, gr = rwkv_w2.shape[1], rwkv_a2.shape[1], rwkv_g2.shape[1]
    assert dr + ir == LANES and gr == LANES
    rw = 3 * d + dr + ir + gr
    gw = 2 * d
    n_grp = router_group_w.shape[-1]
    n_exp = router_expert_w.shape[-1]
    row = lambda a: a.astype(F32).reshape(1, -1)

    head_of_lane = jnp.arange(d, dtype=jnp.int32) // hd
    hsum = (head_of_lane[:, None] == jnp.arange(LANES, dtype=jnp.int32)[None, :]).astype(BF16)
    rt_w = jnp.zeros((d, LANES), F32).at[:, :n_exp].set(router_expert_w[l]).at[:, n_exp:n_exp + n_grp].set(
        router_group_w[l])
    rt_hi = rt_w.astype(BF16)
    rt_lo = (rt_w - rt_hi.astype(F32)).astype(BF16)
    rt_b = jnp.zeros((1, LANES), F32).at[0, :n_exp].set(router_expert_b[l]).at[0, n_exp:n_exp + n_grp].set(
        router_group_b[l])
    w_in_b = w_in[l].astype(BF16)
    w = dict(
        cw=cw, rw=rw, gw=gw, n_exp=n_exp, n_grp=n_grp,
        n1=row(norm1_g[l]), w_in_b=w_in_b, w_rw_b=w_in_b[:, 2 * cw:2 * cw + rw],
        dw=conv_dw_w[l].astype(F32), dwb=row(conv_dw_b[l]), lng=row(conv_ln_g[l]), lnb=row(conv_ln_b[l]),
        conv_out_b16=conv_out_w[l].astype(BF16), conv_out_bias=row(conv_out_b[l]),
        mu=row(rwkv_mu[l]), w0=row(rwkv_w0[l]), a0=row(rwkv_a0[l]),
        w2p=jnp.concatenate([rwkv_w2[l], jnp.zeros((ir, d), F32)], axis=0).astype(BF16),
        a2p=jnp.concatenate([jnp.zeros((dr, d), F32), rwkv_a2[l]], axis=0).astype(BF16),
        g2b=rwkv_g2[l].astype(BF16), k_k=row(rwkv_k_k[l]), k_a=row(rwkv_k_a[l]),
        hsum=hsum, hexp=hsum.T, r_k=row(rwkv_r_k[l]), gn_g=row(rwkv_gn_g[l]), gn_b=row(rwkv_gn_b[l]),
        rwkv_out_b16=rwkv_out_w[l].astype(BF16), w_o_b16=w_o[l].astype(BF16), n2=row(norm2_g[l]),
        rt_hi=rt_hi, rt_lo=rt_lo, rt_b=rt_b,
        wg_b=expert_w_gate[l].astype(BF16), wu_b=expert_w_up[l].astype(BF16), wd_b=expert_w_down[l].astype(BF16),
    )
    fg = row(final_norm_g)
    bp = x_prompt.shape[0]
    taps = conv_dw_w.shape[1]
    yp, cbp, srp, sp = _layer(x_prompt.astype(F32), jnp.zeros((bp, taps - 1, cw), F32), jnp.zeros((bp, d), F32),
                              jnp.zeros((bp, heads, hd, hd), F32), w, fg)
    ys, cbs, srs, ss = _layer(x_sample.astype(F32), state_conv[l], state_shift[l], state_wkv[l], w, fg)
    sdt = state_wkv.dtype
    return (yp.astype(x_prompt.dtype), ys.astype(x_sample.dtype),
            cbp[None].astype(state_conv.dtype), srp[None].astype(state_shift.dtype), sp[None].astype(sdt),
            cbs[None].astype(state_conv.dtype), srs[None].astype(state_shift.dtype), ss[None].astype(sdt))
```

```python
import functools
import math

import jax
import jax.numpy as jnp
from jax import lax
from jax.experimental import pallas as pl
from jax.experimental.pallas import tpu as pltpu

F32 = jnp.float32
BF16 = jnp.bfloat16

NORM_EPS = 1e-6
LN_EPS = 1e-5
GN_EPS = 64e-5
LANES = 128
SUBLANES = 8
MOE_BLOCK = 256
CONV_HALO = 32
VMEM_LIMIT = 56 * 1024 * 1024
WKV_CHUNK = 64
WKV_TIME_BLOCK = 1024
WKV_PAIRS = 2
WKV_UNITS = 8
WKV_UNITS_SHORT = 32
DMA_UNROLL = 8


def _cparams(*sem):
    return pltpu.CompilerParams(dimension_semantics=sem, vmem_limit_bytes=VMEM_LIMIT)


def _dot(a, b):
    return jnp.dot(a, b, preferred_element_type=F32)


def _dot_nt(a, b):
    return lax.dot_general(a, b, (((1,), (1,)), ((), ())), preferred_element_type=F32)


def _dot_tn(a, b):
    return lax.dot_general(a, b, (((0,), (0,)), ((), ())), preferred_element_type=F32)


def _split2(x):
    hi = x.astype(BF16)
    lo = (x - hi.astype(F32)).astype(BF16)
    return hi, lo


def _sigmoid(x):
    return 1.0 / (1.0 + jnp.exp(-x))


def _col_chunks(n, width=512):
    out, c = [], 0
    while c < n:
        w = min(width, n - c)
        out.append((c, w))
        c += w
    return out


def _inproj_kernel(x_ref, g_ref, w_ref, glu_ref, prw_ref, gate_ref, *, cw, rw, gw):
    x = x_ref[...]
    ms = jnp.mean(x * x, axis=-1, keepdims=True)
    ub = (x * lax.rsqrt(ms + NORM_EPS) * g_ref[...]).astype(BF16)
    for c0, w in _col_chunks(cw):
        a = _dot(ub, w_ref[:, c0:c0 + w])
        gt = _dot(ub, w_ref[:, cw + c0:cw + c0 + w])
        glu_ref[:, c0:c0 + w] = a * _sigmoid(gt)
    for c0, w in _col_chunks(rw):
        prw_ref[:, c0:c0 + w] = _dot(ub, w_ref[:, 2 * cw + c0:2 * cw + c0 + w]).astype(BF16)
    for c0, w in _col_chunks(gw):
        s = 2 * cw + rw + c0
        gate_ref[:, c0:c0 + w] = _sigmoid(_dot(ub, w_ref[:, s:s + w])).astype(BF16)


def _inproj(x2, g1, w_in_b, cw, rw, gw, tm):
    m, d = x2.shape
    ncol = w_in_b.shape[1]
    return pl.pallas_call(
        functools.partial(_inproj_kernel, cw=cw, rw=rw, gw=gw),
        out_shape=(jax.ShapeDtypeStruct((m, cw), F32),
                   jax.ShapeDtypeStruct((m, rw), BF16),
                   jax.ShapeDtypeStruct((m, gw), BF16)),
        grid=(m // tm,),
        in_specs=[pl.BlockSpec((tm, d), lambda i: (i, 0)),
                  pl.BlockSpec((1, d), lambda i: (0, 0)),
                  pl.BlockSpec((d, ncol), lambda i: (0, 0), pipeline_mode=pl.Buffered(1))],
        out_specs=(pl.BlockSpec((tm, cw), lambda i: (i, 0)),
                   pl.BlockSpec((tm, rw), lambda i: (i, 0)),
                   pl.BlockSpec((tm, gw), lambda i: (i, 0))),
        compiler_params=_cparams("parallel"),
        name="inproj",
    )(x2, g1, w_in_b)


def _rowproj_kernel(x_ref, w_ref, o_ref):
    o_ref[...] = _dot(x_ref[...].astype(BF16), w_ref[...])


def _rowproj(x2, w_b):
    m, d = x2.shape
    n = w_b.shape[1]
    return pl.pallas_call(
        _rowproj_kernel,
        out_shape=jax.ShapeDtypeStruct((m, n), F32),
        grid=(1,),
        in_specs=[pl.BlockSpec((m, d), lambda i: (0, 0)), pl.BlockSpec((d, n), lambda i: (0, 0))],
        out_specs=pl.BlockSpec((m, n), lambda i: (0, 0)),
        compiler_params=_cparams("arbitrary"),
        name="shift_rowproj",
    )(x2, w_b)


def _rmsnorm_rows_kernel(x_ref, g_ref, o_ref):
    x = x_ref[...]
    ms = jnp.mean(x * x, axis=-1, keepdims=True)
    o_ref[...] = x * lax.rsqrt(ms + NORM_EPS) * g_ref[...]


def _rmsnorm_rows(x2, g1):
    m, d = x2.shape
    return pl.pallas_call(
        _rmsnorm_rows_kernel,
        out_shape=jax.ShapeDtypeStruct((m, d), F32),
        grid=(1,),
        in_specs=[pl.BlockSpec((m, d), lambda i: (0, 0)), pl.BlockSpec((1, d), lambda i: (0, 0))],
        out_specs=pl.BlockSpec((m, d), lambda i: (0, 0)),
        compiler_params=_cparams("arbitrary"),
        name="shift_rmsnorm",
    )(x2, g1)


def _conv_kernel(glu_ref, halo_ref, gate_ref, dw_ref, dwb_ref, lng_ref, lnb_ref, ow_ref, ob_ref,
                 o_ref, full_ref, shift_ref, *, taps):
    nbk, tt, cw = glu_ref.shape
    full_ref[:, 0:CONV_HALO, :] = halo_ref[...]
    full_ref[:, CONV_HALO:CONV_HALO + tt, :] = glu_ref[...]
    span = shift_ref.shape[2]
    for s in range(1, SUBLANES):
        shift_ref[s - 1] = full_ref[:, s:s + span, :]
    first = CONV_HALO - (taps - 1)
    acc = jnp.zeros((nbk, tt, cw), F32) + dwb_ref[...]
    for j in range(taps):
        s = (first + j) % SUBLANES
        q = first + j - s
        win = full_ref[:, q:q + tt, :] if s == 0 else shift_ref[s - 1, :, q:q + tt, :]
        acc = acc + win * dw_ref[j:j + 1, :]
    mean = jnp.mean(acc, axis=-1, keepdims=True)
    cen = acc - mean
    var = jnp.mean(cen * cen, axis=-1, keepdims=True)
    y = cen * lax.rsqrt(var + LN_EPS) * lng_ref[...] + lnb_ref[...]
    y = y * _sigmoid(y)
    y2 = y.reshape(nbk * tt, cw).astype(BF16)
    out = _dot(y2, ow_ref[...]) + ob_ref[...]
    gate = gate_ref[...].reshape(nbk * tt, -1).astype(F32)
    o_ref[...] = (out * gate).reshape(o_ref.shape).astype(o_ref.dtype)


def _conv_branch(glu3, halo3, gates3, dw, dwb, lng, lnb, ow_b, ob, nbk):
    nb, tt, cw = glu3.shape
    d = ow_b.shape[1]
    taps = dw.shape[0]
    full = lambda shape: pl.BlockSpec(shape, lambda i: tuple(0 for _ in shape))
    return pl.pallas_call(
        functools.partial(_conv_kernel, taps=taps),
        out_shape=jax.ShapeDtypeStruct((nb, tt, d), BF16),
        grid=(nb // nbk,),
        in_specs=[pl.BlockSpec((nbk, tt, cw), lambda i: (i, 0, 0)),
                  pl.BlockSpec((nbk, CONV_HALO, cw), lambda i: (i, 0, 0)),
                  pl.BlockSpec((nbk, tt, d), lambda i: (i, 0, 0)),
                  full(dw.shape), full(dwb.shape), full(lng.shape), full(lnb.shape),
                  full(ow_b.shape), full(ob.shape)],
        out_specs=pl.BlockSpec((nbk, tt, d), lambda i: (i, 0, 0)),
        scratch_shapes=[pltpu.VMEM((nbk, CONV_HALO + tt, cw), F32),
                        pltpu.VMEM((SUBLANES - 1, nbk, CONV_HALO - SUBLANES + tt, cw), F32)],
        compiler_params=_cparams("parallel"),
        name="conv_branch",
    )(glu3, halo3, gates3, dw, dwb, lng, lnb, ow_b, ob)


def _prep_kernel(p_ref, prev_ref, mu_ref, w0_ref, w2_ref, a0_ref, a2_ref, g2_ref, kk_ref, ka_ref,
                 hsum_ref, hexp_ref,
                 r_ref, lw_ref, kn_ref, b_ref, v_ref, k_ref, g_ref, sh_ref, *, d):
    nbk, tt, rw = p_ref.shape
    sh_ref[:, 8:8 + tt, :] = p_ref[...].astype(F32)
    sh_ref[:, 7:8, :] = prev_ref[...]
    p = sh_ref[:, 8:8 + tt, :]
    pprev = sh_ref[:, 7:7 + tt, :]
    xm = (p + mu_ref[...] * (pprev - p)).reshape(nbk * tt, rw)
    r = xm[:, 0:d]
    k = xm[:, d:2 * d]
    v = xm[:, 2 * d:3 * d]
    wa = xm[:, 3 * d:3 * d + LANES]
    gd = xm[:, 3 * d + LANES:3 * d + 2 * LANES]
    lw = _dot(jnp.tanh(wa).astype(BF16), w2_ref[...])
    la = _dot(wa.astype(BF16), a2_ref[...])
    logw = -math.exp(-0.5) * _sigmoid(w0_ref[...] + lw)
    a = _sigmoid(a0_ref[...] + la)
    g = _dot(_sigmoid(gd).astype(BF16), g2_ref[...])
    kk = k * kk_ref[...]
    sq_hi, sq_lo = _split2(kk * kk)
    ss = _dot(sq_hi, hsum_ref[...]) + _dot(sq_lo, hsum_ref[...])
    inv = 1.0 / jnp.maximum(jnp.sqrt(ss), 1e-12)
    inv_hi, inv_lo = _split2(inv)
    kn = kk * (_dot(inv_hi, hexp_ref[...]) + _dot(inv_lo, hexp_ref[...]))
    put = lambda ref, val: ref.__setitem__(Ellipsis, val.reshape(ref.shape).astype(ref.dtype))
    put(r_ref, r)
    put(lw_ref, logw)
    put(kn_ref, kn)
    put(b_ref, kn * a)
    put(v_ref, v)
    put(k_ref, k * (1.0 + (a - 1.0) * ka_ref[...]))
    put(g_ref, g)


def _rwkv_prep(prw3, prev3, mu, w0, w2p, a0, a2p, g2b, k_k, k_a, hsum, hexp, nbk):
    nb, tt, rw = prw3.shape
    d = w0.shape[1]
    full = lambda a: pl.BlockSpec(a.shape, lambda i: tuple(0 for _ in a.shape))
    tok = pl.BlockSpec((nbk, tt, d), lambda i: (i, 0, 0))
    o32 = jax.ShapeDtypeStruct((nb, tt, d), F32)
    o16 = jax.ShapeDtypeStruct((nb, tt, d), BF16)
    return pl.pallas_call(
        functools.partial(_prep_kernel, d=d),
        out_shape=(o16, o32, o16, o16, o16, o16, o16),
        grid=(nb // nbk,),
        in_specs=[pl.BlockSpec((nbk, tt, rw), lambda i: (i, 0, 0)),
                  pl.BlockSpec((nbk, 1, rw), lambda i: (i, 0, 0)),
                  full(mu), full(w0), full(w2p), full(a0), full(a2p), full(g2b), full(k_k), full(k_a),
                  full(hsum), full(hexp)],
        out_specs=(tok,) * 7,
        scratch_shapes=[pltpu.VMEM((nbk, 8 + tt, rw), F32)],
        compiler_params=_cparams("parallel"),
        name="rwkv_prep",
    )(prw3, prev3, mu, w0, w2p, a0, a2p, g2b, k_k, k_a, hsum, hexp)


def _wkv_base(c):
    return min(c, 16)


def _wkv_masks(c):
    rows = 2 * c
    ri = jnp.arange(rows, dtype=jnp.int32)[:, None]
    ci = jnp.arange(rows, dtype=jnp.int32)[None, :]
    same = (ri // c) == (ci // c)
    blk = lambda s: (ri // s) == (ci // s)
    out = [-(same & ((ri % c) > (ci % c))).astype(F32), (same & ((ri % c) >= (ci % c))).astype(F32),
           (ri == ci).astype(F32), blk(_wkv_base(c)).astype(F32)]
    s = _wkv_base(c)
    while s < c:
        out.append((blk(2 * s) & jnp.logical_not(blk(s))).astype(F32))
        s *= 2
    return jnp.stack(out)


def _tri_inverse(nabs, cst_ref, c):
    base = _wkv_base(c)
    n = range(len(nabs))
    rows = nabs[0].shape[0]
    nds = [nabs[i] * cst_ref[3] for i in n]
    ps = [cst_ref[2] + nds[i] for i in n]
    xbs = [nds[i].astype(BF16) for i in n]
    levels = int(math.log2(base)) - 1
    xbs = [_dot(xbs[i], xbs[i]).astype(BF16) for i in n]
    for k in range(levels):
        if k == levels - 1:
            ps = [ps[i] + _dot(xbs[i], ps[i].astype(BF16)) for i in n]
        elif rows % LANES == 0:
            rs = [_dot(xbs[i], jnp.concatenate([ps[i].astype(BF16), xbs[i]], axis=1)) for i in n]
            ps = [ps[i] + rs[i][:, 0:rows] for i in n]
            xbs = [rs[i][:, rows:2 * rows].astype(BF16) for i in n]
        else:
            ps = [ps[i] + _dot(xbs[i], ps[i].astype(BF16)) for i in n]
            xbs = [_dot(xbs[i], xbs[i]).astype(BF16) for i in n]
    s, lvl = base, 4
    while s < c:
        low = lambda x: jnp.concatenate([x[g + s:g + 2 * s] for g in range(0, rows, 2 * s)], axis=0)
        pbs = [ps[i].astype(BF16) for i in n]
        ts = [_dot(low(ps[i]).astype(BF16), (nabs[i] * cst_ref[lvl]).astype(BF16)).astype(BF16) for i in n]
        us = [_dot(ts[i], pbs[i]) for i in n]
        ps = [jnp.concatenate([piece for k, g in enumerate(range(0, rows, 2 * s))
                               for piece in (ps[i][g:g + s], ps[i][g + s:g + 2 * s] + us[i][k * s:(k + 1) * s])],
                              axis=0) for i in n]
        s *= 2
        lvl += 1
    return ps


def _wkv_kernel(r_ref, lw_ref, kn_ref, b_ref, v_ref, k_ref, s0_ref, rk_ref, gng_ref, gnb_ref, cst_ref, eye_ref,
                o_ref, so_ref, y1_ref, rp_ref, mxh_ref, mxl_ref, d1_ref, sh_ref, s_ref, *, c, hd, unroll):
    nbk, tb, width = r_ref.shape
    npl = width // LANES
    nch = tb // c
    rows = 2 * c
    ti = pl.program_id(2)
    lane = lax.broadcasted_iota(jnp.int32, (1, LANES), 1)
    m0 = (lane < hd).astype(F32)
    m1 = 1.0 - m0
    tpos = lax.broadcasted_iota(jnp.int32, (c, LANES), 0)
    units = [(bi, p) for bi in range(nbk) for p in range(npl)]

    def stack(x):
        return jnp.concatenate([x * m0, x * m1], axis=0)

    def rows_of(ci):
        if isinstance(ci, int):
            return pl.ds(ci * c, c)
        return pl.ds(pl.multiple_of(ci * c, c), c)

    @pl.when(ti == 0)
    def _():
        z = jnp.zeros((hd, hd), F32)
        for u, (bi, p) in enumerate(units):
            s_ref[u] = jnp.concatenate(
                [jnp.concatenate([s0_ref[bi, 2 * p], z], axis=1),
                 jnp.concatenate([z, s0_ref[bi, 2 * p + 1]], axis=1)], axis=0)

    def phase1(group):
        n = range(len(group))

        def prologue(bi, p, ci):
            sl = rows_of(ci)
            ln = slice(p * LANES, (p + 1) * LANES)
            ld = lambda ref: ref[bi, sl, ln].astype(F32)
            lw = lw_ref[bi, sl, ln]
            cum = lw
            s = 1
            while s < c:
                cum = cum + jnp.where(tpos >= s, pltpu.roll(cum, s, axis=0), 0.0)
                s *= 2
            g_in = jnp.exp(cum)
            g_inv = jnp.exp(-cum)
            g_prev = jnp.exp(cum - lw)
            lq = stack(ld(kn_ref) * g_prev)
            lr = stack(ld(r_ref) * g_in)
            rbb = stack(ld(b_ref) * g_inv).astype(BF16)
            rkb = stack(ld(k_ref) * g_inv).astype(BF16)
            vsb = stack(ld(v_ref)).astype(BF16)
            return dict(g_last=g_in[c - 1:c, :], lq=lq, lr=lr, rbb=rbb, vsb=vsb,
                        lhs=jnp.concatenate([lq, lr], axis=0).astype(BF16),
                        rhs=jnp.concatenate([rbb, rkb], axis=0))

        st = [prologue(bi, p, ci) for (_, bi, p, ci) in group]
        amat = [_dot_nt(st[i]["lhs"], st[i]["rhs"]) for i in n]
        nab = [amat[i][0:rows, 0:rows] * cst_ref[0] for i in n]
        tinv = _tri_inverse(nab, cst_ref, c)
        nmk = [jnp.concatenate([amat[i][0:rows, rows:2 * rows] * cst_ref[0],
                                amat[i][rows:2 * rows, rows:2 * rows] * cst_ref[1]], axis=0).astype(BF16) for i in n]
        mrb = [(amat[i][rows:2 * rows, 0:rows] * cst_ref[1]).astype(BF16) for i in n]
        nm = [_dot(nmk[i], st[i]["vsb"]) for i in n]
        txb = [_dot(tinv[i].astype(BF16),
                    jnp.concatenate([nm[i][0:rows], st[i]["lq"]], axis=1).astype(BF16)).astype(BF16) for i in n]
        mx2 = [_dot(mrb[i], txb[i]) for i in n]
        tn1 = [_dot_tn(txb[i][:, LANES:2 * LANES], st[i]["rbb"]) for i in n]
        tn2 = [_dot_tn(jnp.concatenate([txb[i][:, 0:LANES], st[i]["vsb"]], axis=0), st[i]["rhs"]) for i in n]
        for i, (u, _, _, ci) in enumerate(group):
            y1_ref[u, ci] = nm[i][rows:2 * rows] + mx2[i][:, 0:LANES]
            rp_ref[u, ci] = (st[i]["lr"] - mx2[i][:, LANES:2 * LANES]).astype(BF16)
            mx_hi, mx_lo = _split2((eye_ref[...] - tn1[i]) * st[i]["g_last"])
            mxh_ref[u, ci] = mx_hi
            mxl_ref[u, ci] = mx_lo
            d1_ref[u, ci] = tn2[i] * st[i]["g_last"]

    def phase3(u, bi, p, ci):
        sl = rows_of(ci)
        ln = slice(p * LANES, (p + 1) * LANES)
        ld = lambda ref: ref[bi, sl, ln].astype(F32)
        yst = y1_ref[u, ci] + _dot_nt(rp_ref[u, ci], sh_ref[u, ci])
        hm = jnp.concatenate([jnp.broadcast_to(m0, (c, LANES)), jnp.broadcast_to(m1, (c, LANES))], axis=0)
        mean = jnp.sum(yst, axis=-1, keepdims=True) * (1.0 / hd)
        cen = (yst - mean) * hm
        var = jnp.sum(cen * cen, axis=-1, keepdims=True) * (1.0 / hd)
        yn = cen * lax.rsqrt(var + GN_EPS)
        rkv = stack(ld(r_ref) * ld(k_ref) * rk_ref[:, ln])
        bonus = jnp.sum(rkv, axis=-1, keepdims=True) * stack(ld(v_ref))
        yp = yn[0:c] + yn[c:rows]
        bp = bonus[0:c] + bonus[c:rows]
        o_ref[bi, sl, ln] = (yp * gng_ref[:, ln] + gnb_ref[:, ln] + bp).astype(o_ref.dtype)

    def chunk_loop(fn):
        def body(cg, carry):
            fn([(u, bi, p, cg * unroll + j) for j in range(unroll) for u, (bi, p) in enumerate(units)])
            return carry
        if nch == unroll:
            body(0, 0)
        else:
            lax.fori_loop(0, nch // unroll, body, 0)

    chunk_loop(phase1)

    def phase2(ci, ss):
        out = []
        for u in range(len(units)):
            s_hi, s_lo = _split2(ss[u])
            sh_ref[u, ci] = s_hi
            mh = mxh_ref[u, ci]
            out.append(_dot(s_hi, mh) + _dot(s_lo, mh) + _dot(s_hi, mxl_ref[u, ci]) + d1_ref[u, ci])
        return tuple(out)

    ss = tuple(s_ref[u] for u in range(len(units)))
    if nch == 1:
        ss = phase2(0, ss)
    else:
        ss = lax.fori_loop(0, nch, phase2, ss)
    for u in range(len(units)):
        s_ref[u] = ss[u]

    chunk_loop(lambda group: [phase3(*g) for g in group])

    @pl.when(ti == pl.num_programs(2) - 1)
    def _():
        for u, (bi, p) in enumerate(units):
            sf = s_ref[u]
            so_ref[bi, 2 * p] = sf[0:hd, 0:hd]
            so_ref[bi, 2 * p + 1] = sf[hd:2 * hd, hd:2 * hd]


def _wkv(r, lw, kn, b, v, k, s0, r_k, gn_g, gn_b, c, nbk, tb, npl, unroll):
    bsz, t, d = r.shape
    hd = s0.shape[-1]
    width = npl * LANES
    nch = tb // c
    assert nch % unroll == 0 and t % tb == 0 and d % width == 0
    rows = 2 * c
    nu = nbk * npl
    cst = _wkv_masks(c)
    eye = jnp.eye(LANES, dtype=F32)
    tok = pl.BlockSpec((nbk, tb, width), lambda i, j, tt: (i, tt, j))
    st = pl.BlockSpec((nbk, 2 * npl, hd, hd), lambda i, j, tt: (i, j, 0, 0))
    vec = pl.BlockSpec((1, width), lambda i, j, tt: (0, j))
    return pl.pallas_call(
        functools.partial(_wkv_kernel, c=c, hd=hd, unroll=unroll),
        out_shape=(jax.ShapeDtypeStruct((bsz, t, d), BF16),
                   jax.ShapeDtypeStruct(s0.shape, F32)),
        grid=(bsz // nbk, d // width, t // tb),
        in_specs=[tok, tok, tok, tok, tok, tok, st, vec, vec, vec,
                  pl.BlockSpec(cst.shape, lambda i, j, tt: (0, 0, 0)),
                  pl.BlockSpec(eye.shape, lambda i, j, tt: (0, 0))],
        out_specs=(tok, st),
        scratch_shapes=[pltpu.VMEM((nu, nch, rows, LANES), F32),
                        pltpu.VMEM((nu, nch, rows, LANES), BF16),
                        pltpu.VMEM((nu, nch, LANES, LANES), BF16),
                        pltpu.VMEM((nu, nch, LANES, LANES), BF16),
                        pltpu.VMEM((nu, nch, LANES, LANES), F32),
                        pltpu.VMEM((nu, nch, LANES, LANES), BF16),
                        pltpu.VMEM((nu, LANES, LANES), F32)],
        compiler_params=_cparams("parallel", "parallel", "arbitrary"),
        name="wkv_scan",
    )(r, lw, kn, b, v, k, s0, r_k, gn_g, gn_b, cst, eye)


def _mix_kernel(o_ref, g_ref, za_ref, gb_ref, x_ref, wout_ref, wo_ref, n2_ref, rwh_ref, rwl_ref, rb_ref,
                x1_ref, h_ref, route_ref, *, n_exp, n_grp):
    og = (o_ref[...].astype(F32) * g_ref[...].astype(F32)).astype(BF16)
    yb = _dot(og, wout_ref[...])
    mix = za_ref[...].astype(F32) + gb_ref[...].astype(F32) * yb
    x1 = x_ref[...] + _dot(mix.astype(BF16), wo_ref[...])
    x1_ref[...] = x1
    ms = jnp.mean(x1 * x1, axis=-1, keepdims=True)
    h = x1 * lax.rsqrt(ms + NORM_EPS) * n2_ref[...]
    h_ref[...] = h
    h_hi, h_lo = _split2(h)
    lg = _dot(h_hi, rwh_ref[...]) + _dot(h_lo, rwh_ref[...]) + _dot(h_hi, rwl_ref[...]) + rb_ref[...]
    lane = lax.broadcasted_iota(jnp.int32, lg.shape, 1)
    big = jnp.int32(1 << 20)
    neg = jnp.float32(-jnp.inf)
    gmask = (lane >= n_exp) & (lane < n_exp + n_grp)
    glog = jnp.where(gmask, lg, neg)
    gmax = jnp.max(glog, axis=-1, keepdims=True)
    gsel = jnp.min(jnp.where(glog == gmax, lane - n_exp, big), axis=-1, keepdims=True)
    pg = 1.0 / jnp.sum(jnp.where(gmask, jnp.exp(lg - gmax), 0.0), axis=-1, keepdims=True)
    epg = n_exp // n_grp
    emask = (lane < n_exp) & ((lane // epg) == gsel)
    el = jnp.where(emask, lg, neg)
    m1 = jnp.max(el, axis=-1, keepdims=True)
    i1 = jnp.min(jnp.where(el == m1, lane, big), axis=-1, keepdims=True)
    el2 = jnp.where(lane == i1, neg, el)
    m2 = jnp.max(el2, axis=-1, keepdims=True)
    i2 = jnp.min(jnp.where(el2 == m2, lane, big), axis=-1, keepdims=True)
    e2 = jnp.exp(m2 - m1)
    p1 = 1.0 / (1.0 + e2)
    p2 = e2 * p1
    route = jnp.where(lane == 0, i1.astype(F32),
                      jnp.where(lane == 1, i2.astype(F32),
                                jnp.where(lane == 2, pg * p1, jnp.where(lane == 3, pg * p2, 0.0))))
    route_ref[...] = route


def _mix(o2, g2, za2, gates2, x2, wout_b, wo_b, n2, rw_hi, rw_lo, rbias, n_exp, n_grp, tm):
    m, d = x2.shape
    tokb = pl.BlockSpec((tm, d), lambda i: (i, 0))
    full = lambda a: pl.BlockSpec(a.shape, lambda i: tuple(0 for _ in a.shape))
    return pl.pallas_call(
        functools.partial(_mix_kernel, n_exp=n_exp, n_grp=n_grp),
        out_shape=(jax.ShapeDtypeStruct((m, d), F32), jax.ShapeDtypeStruct((m, d), F32),
                   jax.ShapeDtypeStruct((m, LANES), F32)),
        grid=(m // tm,),
        in_specs=[tokb, tokb, tokb, pl.BlockSpec((tm, d), lambda i: (i, 1)), tokb,
                  full(wout_b), full(wo_b), full(n2), full(rw_hi), full(rw_lo), full(rbias)],
        out_specs=(tokb, tokb, pl.BlockSpec((tm, LANES), lambda i: (i, 0))),
        compiler_params=_cparams("parallel"),
        name="mix_router",
    )(o2, g2, za2, gates2, x2, wout_b, wo_b, n2, rw_hi, rw_lo, rbias)


def _dispatch_kernel(dest_ref, last_ref, nused_ref, h_ref, xb_ref, zero_ref, sem_ref, *, tq, n_exp):
    i = pl.program_id(0)

    @pl.when(i == 0)
    def _():
        zero_ref[...] = jnp.zeros(zero_ref.shape, zero_ref.dtype)
        n_blocks = xb_ref.shape[0] // MOE_BLOCK
        clear = lambda blk: pltpu.make_async_copy(
            zero_ref, xb_ref.at[pl.ds(blk * MOE_BLOCK, MOE_BLOCK)], sem_ref.at[1])
        targets = [(last_ref[e], last_ref[e] >= 0) for e in range(n_exp)]
        targets += [(n_blocks - 1 - k, n_blocks - 1 - k >= nused_ref[0]) for k in range(n_exp)]
        for blk, needed in targets:
            pl.when(needed)(lambda blk=blk: clear(blk).start())
        for blk, needed in targets:
            pl.when(needed)(lambda blk=blk: clear(blk).wait())

    def body(rr, carry):
        t = i * tq + rr
        for j in range(2):
            pltpu.make_async_copy(h_ref.at[pl.ds(rr, 1)], xb_ref.at[pl.ds(dest_ref[2 * t + j], 1)],
                                  sem_ref.at[0]).start()
        return carry

    lax.fori_loop(0, tq, body, 0, unroll=DMA_UNROLL)
    for _ in range(2):
        pltpu.make_async_copy(h_ref, xb_ref.at[pl.ds(0, tq)], sem_ref.at[0]).wait()


def _dispatch_rows(dest, last_block, nused, h2, n_rows, tq):
    m, d = h2.shape
    assert m % tq == 0 and tq <= n_rows
    n_exp = last_block.shape[0]
    grid_spec = pltpu.PrefetchScalarGridSpec(
        num_scalar_prefetch=3,
        grid=(m // tq,),
        in_specs=[pl.BlockSpec((tq, d), lambda i, ds, lb, nu: (i, 0))],
        out_specs=pl.BlockSpec(memory_space=pl.ANY),
        scratch_shapes=[pltpu.VMEM((MOE_BLOCK, d), h2.dtype), pltpu.SemaphoreType.DMA((2,))],
    )
    return pl.pallas_call(
        functools.partial(_dispatch_kernel, tq=tq, n_exp=n_exp),
        out_shape=jax.ShapeDtypeStruct((n_rows, d), h2.dtype),
        grid_spec=grid_spec,
        compiler_params=_cparams("arbitrary"),
        name="moe_dispatch",
    )(dest, last_block, nused, h2)


def _expert_kernel(be_ref, nused_ref, x_ref, wg_ref, wu_ref, wd_ref, o_ref):
    i = pl.program_id(0)

    @pl.when(i < nused_ref[0])
    def _():
        x = x_ref[...].astype(BF16)
        hg = _dot(x, wg_ref[0])
        hu = _dot(x, wu_ref[0])
        act = (hg * _sigmoid(hg) * hu).astype(BF16)
        o_ref[...] = _dot(act, wd_ref[0])

    @pl.when(i >= nused_ref[0])
    def _():
        o_ref[...] = jnp.zeros(o_ref.shape, o_ref.dtype)


def _experts(block_expert, nused, xb, wg_b, wu_b, wd_b):
    p, d = xb.shape
    nb = block_expert.shape[0]
    de = wg_b.shape[-1]
    grid_spec = pltpu.PrefetchScalarGridSpec(
        num_scalar_prefetch=2,
        grid=(nb,),
        in_specs=[pl.BlockSpec((MOE_BLOCK, d), lambda i, be, nu: (jnp.minimum(i, nu[0] - 1), 0)),
                  pl.BlockSpec((1, d, de), lambda i, be, nu: (be[i], 0, 0)),
                  pl.BlockSpec((1, d, de), lambda i, be, nu: (be[i], 0, 0)),
                  pl.BlockSpec((1, de, d), lambda i, be, nu: (be[i], 0, 0))],
        out_specs=pl.BlockSpec((MOE_BLOCK, d), lambda i, be, nu: (i, 0)),
    )
    return pl.pallas_call(
        _expert_kernel,
        out_shape=jax.ShapeDtypeStruct((p, d), F32),
        grid_spec=grid_spec,
        compiler_params=_cparams("arbitrary"),
        name="expert_blocks",
    )(block_expert, nused, xb, wg_b, wu_b, wd_b)


def _combine_kernel(dest_ref, yb_ref, x1_ref, route_ref, fg_ref, o_ref, buf_ref, sem_ref, *, tm):
    i = pl.program_id(0)
    n = pl.num_programs(0)

    def issue(blk, slot):
        def body(rr, carry):
            for j in range(2):
                dst = dest_ref[(blk * tm + rr) * 2 + j]
                pltpu.make_async_copy(yb_ref.at[pl.ds(dst, 1)], buf_ref.at[slot, j, pl.ds(rr, 1)],
                                      sem_ref.at[slot]).start()
            return carry
        lax.fori_loop(0, tm, body, 0, unroll=DMA_UNROLL)

    @pl.when(i == 0)
    def _():
        issue(0, 0)

    @pl.when(i + 1 < n)
    def _():
        issue(i + 1, (i + 1) % 2)

    slot = i % 2
    for j in range(2):
        pltpu.make_async_copy(yb_ref.at[pl.ds(0, tm)], buf_ref.at[slot, j], sem_ref.at[slot]).wait()
    route = route_ref[...]
    x2 = x1_ref[...] + (route[:, 2:3] * buf_ref[slot, 0] + route[:, 3:4] * buf_ref[slot, 1])
    ms = jnp.mean(x2 * x2, axis=-1, keepdims=True)
    o_ref[...] = x2 * lax.rsqrt(ms + NORM_EPS) * fg_ref[...]


def _combine(dest, yb, x1, route, fg, tm):
    m, d = x1.shape
    grid_spec = pltpu.PrefetchScalarGridSpec(
        num_scalar_prefetch=1,
        grid=(m // tm,),
        in_specs=[pl.BlockSpec(memory_space=pl.ANY),
                  pl.BlockSpec((tm, d), lambda i, ds: (i, 0)),
                  pl.BlockSpec((tm, LANES), lambda i, ds: (i, 0)),
                  pl.BlockSpec((1, d), lambda i, ds: (0, 0))],
        out_specs=pl.BlockSpec((tm, d), lambda i, ds: (i, 0)),
        scratch_shapes=[pltpu.VMEM((2, 2, tm, d), F32), pltpu.SemaphoreType.DMA((2,))],
    )
    return pl.pallas_call(
        functools.partial(_combine_kernel, tm=tm),
        out_shape=jax.ShapeDtypeStruct((m, d), F32),
        grid_spec=grid_spec,
        compiler_params=_cparams("arbitrary"),
        name="moe_combine",
    )(dest, yb, x1, route, fg)


def _dispatch_indices(route, n_exp):
    m = route.shape[0]
    e_flat = route[:, 0:2].astype(jnp.int32).reshape(2 * m)
    a = 2 * m
    nb = -(-a // MOE_BLOCK) + n_exp
    onehot = (e_flat[:, None] == jnp.arange(n_exp, dtype=jnp.int32)[None, :]).astype(jnp.int32)
    csum = jnp.cumsum(onehot, axis=0)
    counts = csum[-1]
    padded = ((counts + MOE_BLOCK - 1) // MOE_BLOCK) * MOE_BLOCK
    ends = jnp.cumsum(padded)
    starts = ends - padded
    dest = jnp.sum((csum - 1 + starts[None, :]) * onehot, axis=1)
    blk_start = jnp.arange(nb, dtype=jnp.int32) * MOE_BLOCK
    block_expert = jnp.minimum(jnp.sum((ends[None, :] <= blk_start[:, None]).astype(jnp.int32), axis=1),
                               n_exp - 1)
    nused = (ends[-1] // MOE_BLOCK).astype(jnp.int32).reshape(1)
    last_block = jnp.where(counts == 0, -1, ends // MOE_BLOCK - 1)
    return block_expert, nused, dest.astype(jnp.int32), last_block.astype(jnp.int32), nb * MOE_BLOCK


def _pick_tile(t, pref):
    tile = min(t, pref)
    assert t % tile == 0, (t, tile)
    return tile


def _group_rows(nb, tt, target_rows):
    nbk = max(1, min(nb, target_rows // tt))
    while nb % nbk:
        nbk -= 1
    return nbk


def _layer(x3, conv_buf, shift_row, s0, w, final_g):
    bsz, t, d = x3.shape
    m = bsz * t
    cw, rw, gw = w["cw"], w["rw"], w["gw"]
    taps = w["dw"].shape[0]
    x2 = x3.reshape(m, d)
    tm = _pick_tile(m, 256)
    glu, prw, gates = _inproj(x2, w["n1"], w["w_in_b"], cw, rw, gw, tm)

    tt = _pick_tile(t, 256)
    nt = t // tt
    nbk = _group_rows(bsz * nt, tt, 256)
    glu4 = glu.reshape(bsz, nt, tt, cw)
    buf_pad = jnp.pad(conv_buf.astype(F32), ((0, 0), (CONV_HALO - (taps - 1), 0), (0, 0)))
    if nt > 1:
        assert tt >= CONV_HALO
        halo = jnp.concatenate([buf_pad[:, None], glu4[:, :-1, tt - CONV_HALO:, :]], axis=1)
    else:
        halo = buf_pad[:, None]
    za = _conv_branch(glu4.reshape(bsz * nt, tt, cw), halo.reshape(bsz * nt, CONV_HALO, cw),
                      gates.reshape(bsz * nt, tt, gw), w["dw"], w["dwb"], w["lng"], w["lnb"],
                      w["conv_out_b16"], w["conv_out_bias"], nbk)
    full_seq = jnp.concatenate([conv_buf.astype(F32), glu.reshape(bsz, t, cw)], axis=1)
    new_buf = full_seq[:, -(taps - 1):]

    prev_rw = _rowproj(shift_row.astype(F32), w["w_rw_b"])
    prw4 = prw.reshape(bsz, nt, tt, rw)
    if nt > 1:
        prev = jnp.concatenate([prev_rw[:, None], prw4[:, :-1, tt - 1, :].astype(F32)], axis=1)
    else:
        prev = prev_rw[:, None]
    r, lw, kn, b, v, k, g = _rwkv_prep(prw4.reshape(bsz * nt, tt, rw), prev.reshape(bsz * nt, 1, rw),
                                       w["mu"], w["w0"], w["w2p"], w["a0"], w["a2p"], w["g2b"],
                                       w["k_k"], w["k_a"], w["hsum"], w["hexp"], nbk)
    sh = lambda a: a.reshape(bsz, t, d)
    chunk = _pick_tile(t, WKV_CHUNK)
    tb = _pick_tile(t, WKV_TIME_BLOCK)
    nch = tb // chunk
    unroll = 2 if nch % 2 == 0 else 1
    n_units = WKV_UNITS_SHORT if (nch == 1 and chunk < 16) else WKV_UNITS
    wkv_nbk = _group_rows(bsz, 1, max(1, n_units // (WKV_PAIRS * unroll)))
    o, s_new = _wkv(sh(r), sh(lw), sh(kn), sh(b), sh(v), sh(k), s0.astype(F32), w["r_k"], w["gn_g"],
                    w["gn_b"], chunk, wkv_nbk, tb, WKV_PAIRS, unroll)

    x1, h, route = _mix(o.reshape(m, d), g.reshape(m, d), za.reshape(m, d), gates, x2, w["rwkv_out_b16"],
                        w["w_o_b16"], w["n2"], w["rt_hi"], w["rt_lo"], w["rt_b"], w["n_exp"], w["n_grp"], tm)
    block_expert, nused, dest, last_block, n_rows = _dispatch_indices(route, w["n_exp"])
    xb = _dispatch_rows(dest, last_block, nused, h, n_rows, _pick_tile(m, 512))
    yb = _experts(block_expert, nused, xb, w["wg_b"], w["wu_b"], w["wd_b"])
    y = _combine(dest, yb, x1, route, final_g, _pick_tile(m, 128))
    u_last = _rmsnorm_rows(x3[:, -1, :], w["n1"])
    return y.reshape(bsz, t, d), new_buf, u_last, s_new


def kernel(x_prompt, x_sample, state_conv, state_shift, state_wkv, norm1_g, w_in, conv_dw_w, conv_dw_b,
           conv_ln_g, conv_ln_b, conv_out_w, conv_out_b, rwkv_mu, rwkv_w0, rwkv_w2, rwkv_a0, rwkv_a2,
           rwkv_g2, rwkv_k_k, rwkv_k_a, rwkv_r_k, rwkv_gn_g, rwkv_gn_b, rwkv_out_w, w_o, norm2_g,
           router_group_w, router_group_b, router_expert_w, router_expert_b, expert_w_gate, expert_w_up,
           expert_w_down, final_norm_g):
    depth = norm1_g.shape[0]
    assert depth == 1, "single-layer trunk"
    l = 0
    d = x_prompt.shape[-1]
    cw = conv_out_w.shape[1]
    hd = state_wkv.shape[-1]
    heads = state_wkv.shape[2]
    assert heads * hd == d and 2 * hd == LANES
    dr, ir, gr = rwkv_w2.shape[1], rwkv_a2.shape[1], rwkv_g2.shape[1]
    assert dr + ir == LANES and gr == LANES
    rw = 3 * d + dr + ir + gr
    gw = 2 * d
    n_grp = router_group_w.shape[-1]
    n_exp = router_expert_w.shape[-1]
    row = lambda a: a.astype(F32).reshape(1, -1)

    head_of_lane = jnp.arange(d, dtype=jnp.int32) // hd
    hsum = (head_of_lane[:, None] == jnp.arange(LANES, dtype=jnp.int32)[None, :]).astype(BF16)
    rt_w = jnp.zeros((d, LANES), F32).at[:, :n_exp].set(router_expert_w[l]).at[:, n_exp:n_exp + n_grp].set(
        router_group_w[l])
    rt_hi = rt_w.astype(BF16)
    rt_lo = (rt_w - rt_hi.astype(F32)).astype(BF16)
    rt_b = jnp.zeros((1, LANES), F32).at[0, :n_exp].set(router_expert_b[l]).at[0, n_exp:n_exp + n_grp].set(
        router_group_b[l])
    w_in_b = w_in[l].astype(BF16)
    w = dict(
        cw=cw, rw=rw, gw=gw, n_exp=n_exp, n_grp=n_grp,
        n1=row(norm1_g[l]), w_in_b=w_in_b, w_rw_b=w_in_b[:, 2 * cw:2 * cw + rw],
        dw=conv_dw_w[l].astype(F32), dwb=row(conv_dw_b[l]), lng=row(conv_ln_g[l]), lnb=row(conv_ln_b[l]),
        conv_out_b16=conv_out_w[l].astype(BF16), conv_out_bias=row(conv_out_b[l]),
        mu=row(rwkv_mu[l]), w0=row(rwkv_w0[l]), a0=row(rwkv_a0[l]),
        w2p=jnp.concatenate([rwkv_w2[l], jnp.zeros((ir, d), F32)], axis=0).astype(BF16),
        a2p=jnp.concatenate([jnp.zeros((dr, d), F32), rwkv_a2[l]], axis=0).astype(BF16),
        g2b=rwkv_g2[l].astype(BF16), k_k=row(rwkv_k_k[l]), k_a=row(rwkv_k_a[l]),
        hsum=hsum, hexp=hsum.T, r_k=row(rwkv_r_k[l]), gn_g=row(rwkv_gn_g[l]), gn_b=row(rwkv_gn_b[l]),
        rwkv_out_b16=rwkv_out_w[l].astype(BF16), w_o_b16=w_o[l].astype(BF16), n2=row(norm2_g[l]),
        rt_hi=rt_hi, rt_lo=rt_lo, rt_b=rt_b,
        wg_b=expert_w_gate[l].astype(BF16), wu_b=expert_w_up[l].astype(BF16), wd_b=expert_w_down[l].astype(BF16),
    )
    fg = row(final_norm_g)
    bp = x_prompt.shape[0]
    taps = conv_dw_w.shape[1]
    yp, cbp, srp, sp = _layer(x_prompt.astype(F32), jnp.zeros((bp, taps - 1, cw), F32), jnp.zeros((bp, d), F32),
                              jnp.zeros((bp, heads, hd, hd), F32), w, fg)
    ys, cbs, srs, ss = _layer(x_sample.astype(F32), state_conv[l], state_shift[l], state_wkv[l], w, fg)
    sdt = state_wkv.dtype
    return (yp.astype(x_prompt.dtype), ys.astype(x_sample.dtype),
            cbp[None].astype(state_conv.dtype), srp[None].astype(state_shift.dtype), sp[None].astype(sdt),
            cbs[None].astype(state_conv.dtype), srs[None].astype(state_shift.dtype), ss[None].astype(sdt))
```

```python
import functools
import math

import jax
import jax.numpy as jnp
from jax import lax
from jax.experimental import pallas as pl
from jax.experimental.pallas import tpu as pltpu

F32 = jnp.float32
BF16 = jnp.bfloat16

NORM_EPS = 1e-6
LN_EPS = 1e-5
GN_EPS = 64e-5
LANES = 128
SUBLANES = 8
MOE_BLOCK = 256
CONV_HALO = 32
VMEM_LIMIT = 56 * 1024 * 1024
WKV_CHUNK = 64
WKV_TIME_BLOCK = 512
WKV_PAIRS = 2
WKV_UNITS = 16
WKV_UNITS_SHORT = 32
DMA_UNROLL = 8


def _cparams(*sem):
    return pltpu.CompilerParams(dimension_semantics=sem, vmem_limit_bytes=VMEM_LIMIT)


def _dot(a, b):
    return jnp.dot(a, b, preferred_element_type=F32)


def _dot_nt(a, b):
    return lax.dot_general(a, b, (((1,), (1,)), ((), ())), preferred_element_type=F32)


def _dot_tn(a, b):
    return lax.dot_general(a, b, (((0,), (0,)), ((), ())), preferred_element_type=F32)


def _split2(x):
    hi = x.astype(BF16)
    lo = (x - hi.astype(F32)).astype(BF16)
    return hi, lo


def _sigmoid(x):
    return 1.0 / (1.0 + jnp.exp(-x))


def _col_chunks(n, width=512):
    out, c = [], 0
    while c < n:
        w = min(width, n - c)
        out.append((c, w))
        c += w
    return out


def _inproj_kernel(x_ref, g_ref, w_ref, glu_ref, prw_ref, gate_ref, *, cw, rw, gw):
    x = x_ref[...]
    ms = jnp.mean(x * x, axis=-1, keepdims=True)
    ub = (x * lax.rsqrt(ms + NORM_EPS) * g_ref[...]).astype(BF16)
    for c0, w in _col_chunks(cw):
        a = _dot(ub, w_ref[:, c0:c0 + w])
        gt = _dot(ub, w_ref[:, cw + c0:cw + c0 + w])
        glu_ref[:, c0:c0 + w] = a * _sigmoid(gt)
    for c0, w in _col_chunks(rw):
        prw_ref[:, c0:c0 + w] = _dot(ub, w_ref[:, 2 * cw + c0:2 * cw + c0 + w]).astype(BF16)
    for c0, w in _col_chunks(gw):
        s = 2 * cw + rw + c0
        gate_ref[:, c0:c0 + w] = _sigmoid(_dot(ub, w_ref[:, s:s + w])).astype(BF16)


def _inproj(x2, g1, w_in_b, cw, rw, gw, tm):
    m, d = x2.shape
    ncol = w_in_b.shape[1]
    return pl.pallas_call(
        functools.partial(_inproj_kernel, cw=cw, rw=rw, gw=gw),
        out_shape=(jax.ShapeDtypeStruct((m, cw), F32),
                   jax.ShapeDtypeStruct((m, rw), BF16),
                   jax.ShapeDtypeStruct((m, gw), BF16)),
        grid=(m // tm,),
        in_specs=[pl.BlockSpec((tm, d), lambda i: (i, 0)),
                  pl.BlockSpec((1, d), lambda i: (0, 0)),
                  pl.BlockSpec((d, ncol), lambda i: (0, 0), pipeline_mode=pl.Buffered(1))],
        out_specs=(pl.BlockSpec((tm, cw), lambda i: (i, 0)),
                   pl.BlockSpec((tm, rw), lambda i: (i, 0)),
                   pl.BlockSpec((tm, gw), lambda i: (i, 0))),
        compiler_params=_cparams("parallel"),
        name="inproj",
    )(x2, g1, w_in_b)


def _rowproj_kernel(x_ref, w_ref, o_ref):
    o_ref[...] = _dot(x_ref[...].astype(BF16), w_ref[...])


def _rowproj(x2, w_b):
    m, d = x2.shape
    n = w_b.shape[1]
    return pl.pallas_call(
        _rowproj_kernel,
        out_shape=jax.ShapeDtypeStruct((m, n), F32),
        grid=(1,),
        in_specs=[pl.BlockSpec((m, d), lambda i: (0, 0)), pl.BlockSpec((d, n), lambda i: (0, 0))],
        out_specs=pl.BlockSpec((m, n), lambda i: (0, 0)),
        compiler_params=_cparams("arbitrary"),
        name="shift_rowproj",
    )(x2, w_b)


def _rmsnorm_rows_kernel(x_ref, g_ref, o_ref):
    x = x_ref[...]
    ms = jnp.mean(x * x, axis=-1, keepdims=True)
    o_ref[...] = x * lax.rsqrt(ms + NORM_EPS) * g_ref[...]


def _rmsnorm_rows(x2, g1):
    m, d = x2.shape
    return pl.pallas_call(
        _rmsnorm_rows_kernel,
        out_shape=jax.ShapeDtypeStruct((m, d), F32),
        grid=(1,),
        in_specs=[pl.BlockSpec((m, d), lambda i: (0, 0)), pl.BlockSpec((1, d), lambda i: (0, 0))],
        out_specs=pl.BlockSpec((m, d), lambda i: (0, 0)),
        compiler_params=_cparams("arbitrary"),
        name="shift_rmsnorm",
    )(x2, g1)


def _conv_kernel(glu_ref, halo_ref, gate_ref, dw_ref, dwb_ref, lng_ref, lnb_ref, ow_ref, ob_ref,
                 o_ref, full_ref, shift_ref, *, taps):
    nbk, tt, cw = glu_ref.shape
    full_ref[:, 0:CONV_HALO, :] = halo_ref[...]
    full_ref[:, CONV_HALO:CONV_HALO + tt, :] = glu_ref[...]
    span = shift_ref.shape[2]
    for s in range(1, SUBLANES):
        shift_ref[s - 1] = full_ref[:, s:s + span, :]
    first = CONV_HALO - (taps - 1)
    acc = jnp.zeros((nbk, tt, cw), F32) + dwb_ref[...]
    for j in range(taps):
        s = (first + j) % SUBLANES
        q = first + j - s
        win = full_ref[:, q:q + tt, :] if s == 0 else shift_ref[s - 1, :, q:q + tt, :]
        acc = acc + win * dw_ref[j:j + 1, :]
    mean = jnp.mean(acc, axis=-1, keepdims=True)
    cen = acc - mean
    var = jnp.mean(cen * cen, axis=-1, keepdims=True)
    y = cen * lax.rsqrt(var + LN_EPS) * lng_ref[...] + lnb_ref[...]
    y = y * _sigmoid(y)
    y2 = y.reshape(nbk * tt, cw).astype(BF16)
    out = _dot(y2, ow_ref[...]) + ob_ref[...]
    gate = gate_ref[...].reshape(nbk * tt, -1).astype(F32)
    o_ref[...] = (out * gate).reshape(o_ref.shape).astype(o_ref.dtype)


def _conv_branch(glu3, halo3, gates3, dw, dwb, lng, lnb, ow_b, ob, nbk):
    nb, tt, cw = glu3.shape
    d = ow_b.shape[1]
    taps = dw.shape[0]
    full = lambda shape: pl.BlockSpec(shape, lambda i: tuple(0 for _ in shape))
    return pl.pallas_call(
        functools.partial(_conv_kernel, taps=taps),
        out_shape=jax.ShapeDtypeStruct((nb, tt, d), BF16),
        grid=(nb // nbk,),
        in_specs=[pl.BlockSpec((nbk, tt, cw), lambda i: (i, 0, 0)),
                  pl.BlockSpec((nbk, CONV_HALO, cw), lambda i: (i, 0, 0)),
                  pl.BlockSpec((nbk, tt, d), lambda i: (i, 0, 0)),
                  full(dw.shape), full(dwb.shape), full(lng.shape), full(lnb.shape),
                  full(ow_b.shape), full(ob.shape)],
        out_specs=pl.BlockSpec((nbk, tt, d), lambda i: (i, 0, 0)),
        scratch_shapes=[pltpu.VMEM((nbk, CONV_HALO + tt, cw), F32),
                        pltpu.VMEM((SUBLANES - 1, nbk, CONV_HALO - SUBLANES + tt, cw), F32)],
        compiler_params=_cparams("parallel"),
        name="conv_branch",
    )(glu3, halo3, gates3, dw, dwb, lng, lnb, ow_b, ob)


def _prep_kernel(p_ref, prev_ref, mu_ref, w0_ref, w2_ref, a0_ref, a2_ref, g2_ref, kk_ref, ka_ref,
                 hsum_ref, hexp_ref,
                 shift_ref, r_ref, lw_ref, kn_ref, b_ref, v_ref, k_ref, g_ref, *, d):
    nbk, tt, rw = p_ref.shape
    rows = nbk * tt
    first = (lax.broadcasted_iota(jnp.int32, (rows, 1), 0) % tt) == 0

    def mixed(c0, c1):
        p = p_ref[:, :, c0:c1].astype(F32).reshape(rows, c1 - c0)
        pb = p_ref[0, :, c0:c1] if nbk == 1 else p.astype(BF16)
        prev = jnp.broadcast_to(prev_ref[:, :, c0:c1], (nbk, tt, c1 - c0)).reshape(rows, c1 - c0)
        pprev = jnp.where(first, prev, _dot(shift_ref[...], pb))
        return p + mu_ref[:, c0:c1] * (pprev - p)

    r = mixed(0, d)
    k = mixed(d, 2 * d)
    v = mixed(2 * d, 3 * d)
    wa = mixed(3 * d, 3 * d + LANES)
    gd = mixed(3 * d + LANES, 3 * d + 2 * LANES)
    lw = _dot(jnp.tanh(wa).astype(BF16), w2_ref[...])
    la = _dot(wa.astype(BF16), a2_ref[...])
    logw = -math.exp(-0.5) * _sigmoid(w0_ref[...] + lw)
    a = _sigmoid(a0_ref[...] + la)
    g = _dot(_sigmoid(gd).astype(BF16), g2_ref[...])
    kk = k * kk_ref[...]
    sq_hi, sq_lo = _split2(kk * kk)
    ss = _dot(sq_hi, hsum_ref[...]) + _dot(sq_lo, hsum_ref[...])
    inv = 1.0 / jnp.maximum(jnp.sqrt(ss), 1e-12)
    inv_hi, inv_lo = _split2(inv)
    kn = kk * (_dot(inv_hi, hexp_ref[...]) + _dot(inv_lo, hexp_ref[...]))
    put = lambda ref, val: ref.__setitem__(Ellipsis, val.reshape(ref.shape).astype(ref.dtype))
    put(r_ref, r)
    put(lw_ref, logw)
    put(kn_ref, kn)
    put(b_ref, kn * a)
    put(v_ref, v)
    put(k_ref, k * (1.0 + (a - 1.0) * ka_ref[...]))
    put(g_ref, g)


def _rwkv_prep(prw3, prev3, mu, w0, w2p, a0, a2p, g2b, k_k, k_a, hsum, hexp, nbk):
    nb, tt, rw = prw3.shape
    d = w0.shape[1]
    full = lambda a: pl.BlockSpec(a.shape, lambda i: tuple(0 for _ in a.shape))
    tok = pl.BlockSpec((nbk, tt, d), lambda i: (i, 0, 0))
    o32 = jax.ShapeDtypeStruct((nb, tt, d), F32)
    o16 = jax.ShapeDtypeStruct((nb, tt, d), BF16)
    rows = nbk * tt
    ri = jnp.arange(rows, dtype=jnp.int32)
    shift = ((ri[:, None] - 1 == ri[None, :]) & (ri[:, None] % tt != 0)).astype(BF16)
    return pl.pallas_call(
        functools.partial(_prep_kernel, d=d),
        out_shape=(o16, o32, o16, o16, o16, o16, o16),
        grid=(nb // nbk,),
        in_specs=[pl.BlockSpec((nbk, tt, rw), lambda i: (i, 0, 0)),
                  pl.BlockSpec((nbk, 1, rw), lambda i: (i, 0, 0)),
                  full(mu), full(w0), full(w2p), full(a0), full(a2p), full(g2b), full(k_k), full(k_a),
                  full(hsum), full(hexp), full(shift)],
        out_specs=(tok,) * 7,
        compiler_params=_cparams("parallel"),
        name="rwkv_prep",
    )(prw3, prev3, mu, w0, w2p, a0, a2p, g2b, k_k, k_a, hsum, hexp, shift)


def _wkv_base(c):
    return min(c, 16)


def _wkv_masks(c):
    rows = 2 * c
    ri = jnp.arange(rows, dtype=jnp.int32)[:, None]
    ci = jnp.arange(rows, dtype=jnp.int32)[None, :]
    same = (ri // c) == (ci // c)
    blk = lambda s: (ri // s) == (ci // s)
    out = [-(same & ((ri % c) > (ci % c))).astype(F32), (same & ((ri % c) >= (ci % c))).astype(F32),
           (ri == ci).astype(F32), blk(_wkv_base(c)).astype(F32)]
    s = _wkv_base(c)
    while s < c:
        out.append((blk(2 * s) & jnp.logical_not(blk(s))).astype(F32))
        s *= 2
    return jnp.stack(out)


def _tri_inverse(nabs, cst_ref, c):
    base = _wkv_base(c)
    n = range(len(nabs))
    rows = nabs[0].shape[0]
    nds = [nabs[i] * cst_ref[3] for i in n]
    ps = [cst_ref[2] + nds[i] for i in n]
    xbs = [nds[i].astype(BF16) for i in n]
    levels = int(math.log2(base)) - 1
    xbs = [_dot(xbs[i], xbs[i]).astype(BF16) for i in n]
    for k in range(levels):
        if k == levels - 1:
            ps = [ps[i] + _dot(xbs[i], ps[i].astype(BF16)) for i in n]
        elif rows % LANES == 0:
            rs = [_dot(xbs[i], jnp.concatenate([ps[i].astype(BF16), xbs[i]], axis=1)) for i in n]
            ps = [ps[i] + rs[i][:, 0:rows] for i in n]
            xbs = [rs[i][:, rows:2 * rows].astype(BF16) for i in n]
        else:
            ps = [ps[i] + _dot(xbs[i], ps[i].astype(BF16)) for i in n]
            xbs = [_dot(xbs[i], xbs[i]).astype(BF16) for i in n]
    s, lvl = base, 4
    while s < c:
        low = lambda x: jnp.concatenate([x[g + s:g + 2 * s] for g in range(0, rows, 2 * s)], axis=0)
        pbs = [ps[i].astype(BF16) for i in n]
        ts = [_dot(low(ps[i]).astype(BF16), (nabs[i] * cst_ref[lvl]).astype(BF16)).astype(BF16) for i in n]
        us = [_dot(ts[i], pbs[i]) for i in n]
        ps = [jnp.concatenate([piece for k, g in enumerate(range(0, rows, 2 * s))
                               for piece in (ps[i][g:g + s], ps[i][g + s:g + 2 * s] + us[i][k * s:(k + 1) * s])],
                              axis=0) for i in n]
        s *= 2
        lvl += 1
    return ps


def _wkv_kernel(r_ref, lw_ref, kn_ref, b_ref, v_ref, k_ref, s0_ref, rk_ref, gng_ref, gnb_ref, cst_ref, eye_ref,
                o_ref, so_ref, y1_ref, rp_ref, mxh_ref, mxl_ref, d1_ref, sh_ref, s_ref, *, c, hd, unroll):
    nbk, tb, width = r_ref.shape
    npl = width // LANES
    nch = tb // c
    rows = 2 * c
    ti = pl.program_id(2)
    lane = lax.broadcasted_iota(jnp.int32, (1, LANES), 1)
    m0 = (lane < hd).astype(F32)
    m1 = 1.0 - m0
    tpos = lax.broadcasted_iota(jnp.int32, (c, LANES), 0)
    units = [(bi, p) for bi in range(nbk) for p in range(npl)]

    def stack(x):
        return jnp.concatenate([x * m0, x * m1], axis=0)

    def rows_of(ci):
        if isinstance(ci, int):
            return pl.ds(ci * c, c)
        return pl.ds(pl.multiple_of(ci * c, c), c)

    @pl.when(ti == 0)
    def _():
        z = jnp.zeros((hd, hd), F32)
        for u, (bi, p) in enumerate(units):
            s_ref[u] = jnp.concatenate(
                [jnp.concatenate([s0_ref[bi, 2 * p], z], axis=1),
                 jnp.concatenate([z, s0_ref[bi, 2 * p + 1]], axis=1)], axis=0)

    def phase1(group):
        n = range(len(group))

        def prologue(bi, p, ci):
            sl = rows_of(ci)
            ln = slice(p * LANES, (p + 1) * LANES)
            ld = lambda ref: ref[bi, sl, ln].astype(F32)
            lw = lw_ref[bi, sl, ln]
            cum = lw
            s = 1
            while s < c:
                cum = cum + jnp.where(tpos >= s, pltpu.roll(cum, s, axis=0), 0.0)
                s *= 2
            g_in = jnp.exp(cum)
            g_inv = jnp.exp(-cum)
            g_prev = jnp.exp(cum - lw)
            lq = stack(ld(kn_ref) * g_prev)
            lr = stack(ld(r_ref) * g_in)
            rbb = stack(ld(b_ref) * g_inv).astype(BF16)
            rkb = stack(ld(k_ref) * g_inv).astype(BF16)
            vsb = stack(ld(v_ref)).astype(BF16)
            return dict(g_last=g_in[c - 1:c, :], lq=lq, lr=lr, rbb=rbb, vsb=vsb,
                        lhs=jnp.concatenate([lq, lr], axis=0).astype(BF16),
                        rhs=jnp.concatenate([rbb, rkb], axis=0))

        st = [prologue(bi, p, ci) for (_, bi, p, ci) in group]
        amat = [_dot_nt(st[i]["lhs"], st[i]["rhs"]) for i in n]
        nab = [amat[i][0:rows, 0:rows] * cst_ref[0] for i in n]
        tinv = _tri_inverse(nab, cst_ref, c)
        nmk = [jnp.concatenate([amat[i][0:rows, rows:2 * rows] * cst_ref[0],
                                amat[i][rows:2 * rows, rows:2 * rows] * cst_ref[1]], axis=0).astype(BF16) for i in n]
        mrb = [(amat[i][rows:2 * rows, 0:rows] * cst_ref[1]).astype(BF16) for i in n]
        nm = [_dot(nmk[i], st[i]["vsb"]) for i in n]
        txb = [_dot(tinv[i].astype(BF16),
                    jnp.concatenate([nm[i][0:rows], st[i]["lq"]], axis=1).astype(BF16)).astype(BF16) for i in n]
        mx2 = [_dot(mrb[i], txb[i]) for i in n]
        tn1 = [_dot_tn(txb[i][:, LANES:2 * LANES], st[i]["rbb"]) for i in n]
        tn2 = [_dot_tn(jnp.concatenate([txb[i][:, 0:LANES], st[i]["vsb"]], axis=0), st[i]["rhs"]) for i in n]
        for i, (u, _, _, ci) in enumerate(group):
            y1_ref[u, ci] = nm[i][rows:2 * rows] + mx2[i][:, 0:LANES]
            rp_ref[u, ci] = (st[i]["lr"] - mx2[i][:, LANES:2 * LANES]).astype(BF16)
            mx_hi, mx_lo = _split2((eye_ref[...] - tn1[i]) * st[i]["g_last"])
            mxh_ref[u, ci] = mx_hi
            mxl_ref[u, ci] = mx_lo
            d1_ref[u, ci] = tn2[i] * st[i]["g_last"]

    def phase3(u, bi, p, ci):
        sl = rows_of(ci)
        ln = slice(p * LANES, (p + 1) * LANES)
        ld = lambda ref: ref[bi, sl, ln].astype(F32)
        yst = y1_ref[u, ci] + _dot_nt(rp_ref[u, ci], sh_ref[u, ci])
        hm = jnp.concatenate([jnp.broadcast_to(m0, (c, LANES)), jnp.broadcast_to(m1, (c, LANES))], axis=0)
        mean = jnp.sum(yst, axis=-1, keepdims=True) * (1.0 / hd)
        cen = (yst - mean) * hm
        var = jnp.sum(cen * cen, axis=-1, keepdims=True) * (1.0 / hd)
        yn = cen * lax.rsqrt(var + GN_EPS)
        rkv = stack(ld(r_ref) * ld(k_ref) * rk_ref[:, ln])
        bonus = jnp.sum(rkv, axis=-1, keepdims=True) * stack(ld(v_ref))
        yp = yn[0:c] + yn[c:rows]
        bp = bonus[0:c] + bonus[c:rows]
        o_ref[bi, sl, ln] = (yp * gng_ref[:, ln] + gnb_ref[:, ln] + bp).astype(o_ref.dtype)

    def chunk_loop(fn, per_body):
        def body(cg, carry):
            fn([(u, bi, p, cg * per_body + j) for j in range(per_body) for u, (bi, p) in enumerate(units)])
            return carry
        if nch == per_body:
            body(0, 0)
        else:
            lax.fori_loop(0, nch // per_body, body, 0)

    chunk_loop(phase1, unroll)

    def phase2(ci, ss):
        out = []
        for u in range(len(units)):
            s_hi, s_lo = _split2(ss[u])
            sh_ref[u, ci] = s_hi
            mh = mxh_ref[u, ci]
            out.append(_dot(s_hi, mh) + _dot(s_lo, mh) + _dot(s_hi, mxl_ref[u, ci]) + d1_ref[u, ci])
        return tuple(out)

    ss = tuple(s_ref[u] for u in range(len(units)))
    if nch == 1:
        ss = phase2(0, ss)
    else:
        ss = lax.fori_loop(0, nch, phase2, ss)
    for u in range(len(units)):
        s_ref[u] = ss[u]

    chunk_loop(lambda group: [phase3(*g) for g in group], 1)

    @pl.when(ti == pl.num_programs(2) - 1)
    def _():
        for u, (bi, p) in enumerate(units):
            sf = s_ref[u]
            so_ref[bi, 2 * p] = sf[0:hd, 0:hd]
            so_ref[bi, 2 * p + 1] = sf[hd:2 * hd, hd:2 * hd]


def _wkv(r, lw, kn, b, v, k, s0, r_k, gn_g, gn_b, c, nbk, tb, npl, unroll):
    bsz, t, d = r.shape
    hd = s0.shape[-1]
    width = npl * LANES
    nch = tb // c
    assert nch % unroll == 0 and t % tb == 0 and d % width == 0
    rows = 2 * c
    nu = nbk * npl
    cst = _wkv_masks(c)
    eye = jnp.eye(LANES, dtype=F32)
    tok = pl.BlockSpec((nbk, tb, width), lambda i, j, tt: (i, tt, j))
    st = pl.BlockSpec((nbk, 2 * npl, hd, hd), lambda i, j, tt: (i, j, 0, 0))
    vec = pl.BlockSpec((1, width), lambda i, j, tt: (0, j))
    return pl.pallas_call(
        functools.partial(_wkv_kernel, c=c, hd=hd, unroll=unroll),
        out_shape=(jax.ShapeDtypeStruct((bsz, t, d), BF16),
                   jax.ShapeDtypeStruct(s0.shape, F32)),
        grid=(bsz // nbk, d // width, t // tb),
        in_specs=[tok, tok, tok, tok, tok, tok, st, vec, vec, vec,
                  pl.BlockSpec(cst.shape, lambda i, j, tt: (0, 0, 0)),
                  pl.BlockSpec(eye.shape, lambda i, j, tt: (0, 0))],
        out_specs=(tok, st),
        scratch_shapes=[pltpu.VMEM((nu, nch, rows, LANES), F32),
                        pltpu.VMEM((nu, nch, rows, LANES), BF16),
                        pltpu.VMEM((nu, nch, LANES, LANES), BF16),
                        pltpu.VMEM((nu, nch, LANES, LANES), BF16),
                        pltpu.VMEM((nu, nch, LANES, LANES), F32),
                        pltpu.VMEM((nu, nch, LANES, LANES), BF16),
                        pltpu.VMEM((nu, LANES, LANES), F32)],
        compiler_params=_cparams("parallel", "parallel", "arbitrary"),
        name="wkv_scan",
    )(r, lw, kn, b, v, k, s0, r_k, gn_g, gn_b, cst, eye)


def _mix_kernel(o_ref, g_ref, za_ref, gb_ref, x_ref, wout_ref, wo_ref, n2_ref, rwh_ref, rwl_ref, rb_ref,
                x1_ref, h_ref, route_ref, *, n_exp, n_grp):
    og = (o_ref[...].astype(F32) * g_ref[...].astype(F32)).astype(BF16)
    yb = _dot(og, wout_ref[...])
    mix = za_ref[...].astype(F32) + gb_ref[...].astype(F32) * yb
    x1 = x_ref[...] + _dot(mix.astype(BF16), wo_ref[...])
    x1_ref[...] = x1
    ms = jnp.mean(x1 * x1, axis=-1, keepdims=True)
    h = x1 * lax.rsqrt(ms + NORM_EPS) * n2_ref[...]
    h_ref[...] = h
    h_hi, h_lo = _split2(h)
    lg = _dot(h_hi, rwh_ref[...]) + _dot(h_lo, rwh_ref[...]) + _dot(h_hi, rwl_ref[...]) + rb_ref[...]
    lane = lax.broadcasted_iota(jnp.int32, lg.shape, 1)
    big = jnp.int32(1 << 20)
    neg = jnp.float32(-jnp.inf)
    gmask = (lane >= n_exp) & (lane < n_exp + n_grp)
    glog = jnp.where(gmask, lg, neg)
    gmax = jnp.max(glog, axis=-1, keepdims=True)
    gsel = jnp.min(jnp.where(glog == gmax, lane - n_exp, big), axis=-1, keepdims=True)
    pg = 1.0 / jnp.sum(jnp.where(gmask, jnp.exp(lg - gmax), 0.0), axis=-1, keepdims=True)
    epg = n_exp // n_grp
    emask = (lane < n_exp) & ((lane // epg) == gsel)
    el = jnp.where(emask, lg, neg)
    m1 = jnp.max(el, axis=-1, keepdims=True)
    i1 = jnp.min(jnp.where(el == m1, lane, big), axis=-1, keepdims=True)
    el2 = jnp.where(lane == i1, neg, el)
    m2 = jnp.max(el2, axis=-1, keepdims=True)
    i2 = jnp.min(jnp.where(el2 == m2, lane, big), axis=-1, keepdims=True)
    e2 = jnp.exp(m2 - m1)
    p1 = 1.0 / (1.0 + e2)
    p2 = e2 * p1
    route = jnp.where(lane == 0, i1.astype(F32),
                      jnp.where(lane == 1, i2.astype(F32),
                                jnp.where(lane == 2, pg * p1, jnp.where(lane == 3, pg * p2, 0.0))))
    route_ref[...] = route


def _mix_tiled_kernel(o_ref, g_ref, za_ref, gb_ref, x_ref, wout_ref, wo_ref, n2_ref, rwh_ref, rwl_ref, rb_ref,
                      x1_ref, h_ref, route_ref, *, sub, n_exp, n_grp):
    for r0 in range(0, x_ref.shape[0], sub):
        rows = lambda ref: ref.at[pl.ds(r0, sub)]
        _mix_kernel(rows(o_ref), rows(g_ref), rows(za_ref), rows(gb_ref), rows(x_ref), wout_ref, wo_ref, n2_ref,
                    rwh_ref, rwl_ref, rb_ref, rows(x1_ref), rows(h_ref), rows(route_ref),
                    n_exp=n_exp, n_grp=n_grp)


def _mix(o2, g2, za2, gates2, x2, wout_b, wo_b, n2, rw_hi, rw_lo, rbias, n_exp, n_grp, tm, sub):
    m, d = x2.shape
    tokb = pl.BlockSpec((tm, d), lambda i: (i, 0))
    full = lambda a: pl.BlockSpec(a.shape, lambda i: tuple(0 for _ in a.shape))
    return pl.pallas_call(
        functools.partial(_mix_tiled_kernel, sub=sub, n_exp=n_exp, n_grp=n_grp),
        out_shape=(jax.ShapeDtypeStruct((m, d), F32), jax.ShapeDtypeStruct((m, d), F32),
                   jax.ShapeDtypeStruct((m, LANES), F32)),
        grid=(m // tm,),
        in_specs=[tokb, tokb, tokb, pl.BlockSpec((tm, d), lambda i: (i, 1)), tokb,
                  full(wout_b), full(wo_b), full(n2), full(rw_hi), full(rw_lo), full(rbias)],
        out_specs=(tokb, tokb, pl.BlockSpec((tm, LANES), lambda i: (i, 0))),
        compiler_params=_cparams("parallel"),
        name="mix_router",
    )(o2, g2, za2, gates2, x2, wout_b, wo_b, n2, rw_hi, rw_lo, rbias)


def _dispatch_kernel(dest_ref, last_ref, nused_ref, h_ref, xb_ref, zero_ref, sem_ref, *, tq, n_exp):
    i = pl.program_id(0)

    @pl.when(i == 0)
    def _():
        zero_ref[...] = jnp.zeros(zero_ref.shape, zero_ref.dtype)
        n_blocks = xb_ref.shape[0] // MOE_BLOCK
        clear = lambda blk: pltpu.make_async_copy(
            zero_ref, xb_ref.at[pl.ds(blk * MOE_BLOCK, MOE_BLOCK)], sem_ref.at[1])
        targets = [(last_ref[e], last_ref[e] >= 0) for e in range(n_exp)]
        targets += [(n_blocks - 1 - k, n_blocks - 1 - k >= nused_ref[0]) for k in range(n_exp)]
        for blk, needed in targets:
            pl.when(needed)(lambda blk=blk: clear(blk).start())
        for blk, needed in targets:
            pl.when(needed)(lambda blk=blk: clear(blk).wait())

    def body(rr, carry):
        t = i * tq + rr
        for j in range(2):
            pltpu.make_async_copy(h_ref.at[pl.ds(rr, 1)], xb_ref.at[pl.ds(dest_ref[2 * t + j], 1)],
                                  sem_ref.at[0]).start()
        return carry

    lax.fori_loop(0, tq, body, 0, unroll=DMA_UNROLL)
    for _ in range(2):
        pltpu.make_async_copy(h_ref, xb_ref.at[pl.ds(0, tq)], sem_ref.at[0]).wait()


def _dispatch_rows(dest, last_block, nused, h2, n_rows, tq):
    m, d = h2.shape
    assert m % tq == 0 and tq <= n_rows
    n_exp = last_block.shape[0]
    grid_spec = pltpu.PrefetchScalarGridSpec(
        num_scalar_prefetch=3,
        grid=(m // tq,),
        in_specs=[pl.BlockSpec((tq, d), lambda i, ds, lb, nu: (i, 0))],
        out_specs=pl.BlockSpec(memory_space=pl.ANY),
        scratch_shapes=[pltpu.VMEM((MOE_BLOCK, d), h2.dtype), pltpu.SemaphoreType.DMA((2,))],
    )
    return pl.pallas_call(
        functools.partial(_dispatch_kernel, tq=tq, n_exp=n_exp),
        out_shape=jax.ShapeDtypeStruct((n_rows, d), h2.dtype),
        grid_spec=grid_spec,
        compiler_params=_cparams("arbitrary"),
        name="moe_dispatch",
    )(dest, last_block, nused, h2)


def _expert_kernel(be_ref, nused_ref, x_ref, wg_ref, wu_ref, wd_ref, o_ref):
    i = pl.program_id(0)

    @pl.when(i < nused_ref[0])
    def _():
        x = x_ref[...].astype(BF16)
        hg = _dot(x, wg_ref[0].astype(BF16))
        hu = _dot(x, wu_ref[0].astype(BF16))
        act = (hg * _sigmoid(hg) * hu).astype(BF16)
        o_ref[...] = _dot(act, wd_ref[0].astype(BF16))

    @pl.when(i >= nused_ref[0])
    def _():
        o_ref[...] = jnp.zeros(o_ref.shape, o_ref.dtype)


def _experts(block_expert, nused, xb, wg_b, wu_b, wd_b):
    p, d = xb.shape
    nb = block_expert.shape[0]
    de = wg_b.shape[-1]
    grid_spec = pltpu.PrefetchScalarGridSpec(
        num_scalar_prefetch=2,
        grid=(nb,),
        in_specs=[pl.BlockSpec((MOE_BLOCK, d), lambda i, be, nu: (jnp.minimum(i, nu[0] - 1), 0)),
                  pl.BlockSpec((1, d, de), lambda i, be, nu: (be[i], 0, 0)),
                  pl.BlockSpec((1, d, de), lambda i, be, nu: (be[i], 0, 0)),
                  pl.BlockSpec((1, de, d), lambda i, be, nu: (be[i], 0, 0))],
        out_specs=pl.BlockSpec((MOE_BLOCK, d), lambda i, be, nu: (i, 0)),
    )
    return pl.pallas_call(
        _expert_kernel,
        out_shape=jax.ShapeDtypeStruct((p, d), F32),
        grid_spec=grid_spec,
        compiler_params=_cparams("arbitrary"),
        name="expert_blocks",
    )(block_expert, nused, xb, wg_b, wu_b, wd_b)


def _combine_kernel(dest_ref, yb_ref, x1_ref, route_ref, fg_ref, o_ref, buf_ref, sem_ref, *, tm):
    i = pl.program_id(0)
    n = pl.num_programs(0)

    def issue(blk, slot):
        def body(rr, carry):
            for j in range(2):
                dst = dest_ref[(blk * tm + rr) * 2 + j]
                pltpu.make_async_copy(yb_ref.at[pl.ds(dst, 1)], buf_ref.at[slot, j, pl.ds(rr, 1)],
                                      sem_ref.at[slot]).start()
            return carry
        lax.fori_loop(0, tm, body, 0, unroll=DMA_UNROLL)

    @pl.when(i == 0)
    def _():
        issue(0, 0)

    @pl.when(i + 1 < n)
    def _():
        issue(i + 1, (i + 1) % 2)

    slot = i % 2
    for j in range(2):
        pltpu.make_async_copy(yb_ref.at[pl.ds(0, tm)], buf_ref.at[slot, j], sem_ref.at[slot]).wait()
    route = route_ref[...]
    x2 = x1_ref[...] + (route[:, 2:3] * buf_ref[slot, 0] + route[:, 3:4] * buf_ref[slot, 1])
    ms = jnp.mean(x2 * x2, axis=-1, keepdims=True)
    o_ref[...] = x2 * lax.rsqrt(ms + NORM_EPS) * fg_ref[...]


def _combine(dest, yb, x1, route, fg, tm):
    m, d = x1.shape
    grid_spec = pltpu.PrefetchScalarGridSpec(
        num_scalar_prefetch=1,
        grid=(m // tm,),
        in_specs=[pl.BlockSpec(memory_space=pl.ANY),
                  pl.BlockSpec((tm, d), lambda i, ds: (i, 0)),
                  pl.BlockSpec((tm, LANES), lambda i, ds: (i, 0)),
                  pl.BlockSpec((1, d), lambda i, ds: (0, 0))],
        out_specs=pl.BlockSpec((tm, d), lambda i, ds: (i, 0)),
        scratch_shapes=[pltpu.VMEM((2, 2, tm, d), F32), pltpu.SemaphoreType.DMA((2,))],
    )
    return pl.pallas_call(
        functools.partial(_combine_kernel, tm=tm),
        out_shape=jax.ShapeDtypeStruct((m, d), F32),
        grid_spec=grid_spec,
        compiler_params=_cparams("arbitrary"),
        name="moe_combine",
    )(dest, yb, x1, route, fg)


def _dispatch_indices(route, n_exp):
    m = route.shape[0]
    e_flat = route[:, 0:2].astype(jnp.int32).reshape(2 * m)
    a = 2 * m
    nb = -(-a // MOE_BLOCK) + n_exp
    onehot = (e_flat[:, None] == jnp.arange(n_exp, dtype=jnp.int32)[None, :]).astype(jnp.int32)
    csum = jnp.cumsum(onehot, axis=0)
    counts = csum[-1]
    padded = ((counts + MOE_BLOCK - 1) // MOE_BLOCK) * MOE_BLOCK
    ends = jnp.cumsum(padded)
    starts = ends - padded
    dest = jnp.sum((csum - 1 + starts[None, :]) * onehot, axis=1)
    blk_start = jnp.arange(nb, dtype=jnp.int32) * MOE_BLOCK
    block_expert = jnp.minimum(jnp.sum((ends[None, :] <= blk_start[:, None]).astype(jnp.int32), axis=1),
                               n_exp - 1)
    nused = (ends[-1] // MOE_BLOCK).astype(jnp.int32).reshape(1)
    last_block = jnp.where(counts == 0, -1, ends // MOE_BLOCK - 1)
    return block_expert, nused, dest.astype(jnp.int32), last_block.astype(jnp.int32), nb * MOE_BLOCK


def _pick_tile(t, pref):
    tile = min(t, pref)
    assert t % tile == 0, (t, tile)
    return tile


def _group_rows(nb, tt, target_rows):
    nbk = max(1, min(nb, target_rows // tt))
    while nb % nbk:
        nbk -= 1
    return nbk


def _layer(x3, conv_buf, shift_row, s0, w, final_g):
    bsz, t, d = x3.shape
    m = bsz * t
    cw, rw, gw = w["cw"], w["rw"], w["gw"]
    taps = w["dw"].shape[0]
    x2 = x3.reshape(m, d)
    tm = _pick_tile(m, 256)
    glu, prw, gates = _inproj(x2, w["n1"], w["w_in_b"], cw, rw, gw, tm)

    tt = _pick_tile(t, 256)
    nt = t // tt
    nbk = _group_rows(bsz * nt, tt, 256)
    glu4 = glu.reshape(bsz, nt, tt, cw)
    buf_pad = jnp.pad(conv_buf.astype(F32), ((0, 0), (CONV_HALO - (taps - 1), 0), (0, 0)))
    if nt > 1:
        assert tt >= CONV_HALO
        halo = jnp.concatenate([buf_pad[:, None], glu4[:, :-1, tt - CONV_HALO:, :]], axis=1)
    else:
        halo = buf_pad[:, None]
    za = _conv_branch(glu4.reshape(bsz * nt, tt, cw), halo.reshape(bsz * nt, CONV_HALO, cw),
                      gates.reshape(bsz * nt, tt, gw), w["dw"], w["dwb"], w["lng"], w["lnb"],
                      w["conv_out_b16"], w["conv_out_bias"], nbk)
    full_seq = jnp.concatenate([conv_buf.astype(F32), glu.reshape(bsz, t, cw)], axis=1)
    new_buf = full_seq[:, -(taps - 1):]

    prev_rw = _rowproj(shift_row.astype(F32), w["w_rw_b"])
    prw4 = prw.reshape(bsz, nt, tt, rw)
    if nt > 1:
        prev = jnp.concatenate([prev_rw[:, None], prw4[:, :-1, tt - 1, :].astype(F32)], axis=1)
    else:
        prev = prev_rw[:, None]
    r, lw, kn, b, v, k, g = _rwkv_prep(prw4.reshape(bsz * nt, tt, rw), prev.reshape(bsz * nt, 1, rw),
                                       w["mu"], w["w0"], w["w2p"], w["a0"], w["a2p"], w["g2b"],
                                       w["k_k"], w["k_a"], w["hsum"], w["hexp"], nbk)
    sh = lambda a: a.reshape(bsz, t, d)
    chunk = _pick_tile(t, WKV_CHUNK)
    tb = _pick_tile(t, WKV_TIME_BLOCK)
    nch = tb // chunk
    unroll = 2 if nch % 2 == 0 else 1
    n_units = WKV_UNITS_SHORT if (nch == 1 and chunk < 16) else WKV_UNITS
    wkv_nbk = _group_rows(bsz, 1, max(1, n_units // (WKV_PAIRS * unroll)))
    o, s_new = _wkv(sh(r), sh(lw), sh(kn), sh(b), sh(v), sh(k), s0.astype(F32), w["r_k"], w["gn_g"],
                    w["gn_b"], chunk, wkv_nbk, tb, WKV_PAIRS, unroll)

    x1, h, route = _mix(o.reshape(m, d), g.reshape(m, d), za.reshape(m, d), gates, x2, w["rwkv_out_b16"],
                        w["w_o_b16"], w["n2"], w["rt_hi"], w["rt_lo"], w["rt_b"], w["n_exp"], w["n_grp"],
                        _pick_tile(m, 2 * tm), tm)
    block_expert, nused, dest, last_block, n_rows = _dispatch_indices(route, w["n_exp"])
    xb = _dispatch_rows(dest, last_block, nused, h, n_rows, _pick_tile(m, 512))
    yb = _experts(block_expert, nused, xb, w["wg_b"], w["wu_b"], w["wd_b"])
    y = _combine(dest, yb, x1, route, final_g, _pick_tile(m, 128))
    u_last = _rmsnorm_rows(x3[:, -1, :], w["n1"])
    return y.reshape(bsz, t, d), new_buf, u_last, s_new


def kernel(x_prompt, x_sample, state_conv, state_shift, state_wkv, norm1_g, w_in, conv_dw_w, conv_dw_b,
           conv_ln_g, conv_ln_b, conv_out_w, conv_out_b, rwkv_mu, rwkv_w0, rwkv_w2, rwkv_a0, rwkv_a2,
           rwkv_g2, rwkv_k_k, rwkv_k_a, rwkv_r_k, rwkv_gn_g, rwkv_gn_b, rwkv_out_w, w_o, norm2_g,
           router_group_w, router_group_b, router_expert_w, router_expert_b, expert_w_gate, expert_w_up,
           expert_w_down, final_norm_g):
    depth = norm1_g.shape[0]
    assert depth == 1, "single-layer trunk"
    l = 0
    d = x_prompt.shape[-1]
    cw = conv_out_w.shape[1]
    hd = state_wkv.shape[-1]
    heads = state_wkv.shape[2]
    assert heads * hd == d and 2 * hd == LANES
    dr, ir, gr = rwkv_w2.shape[1], rwkv_a2.shape[1], rwkv_g2.shape[1]
    assert dr + ir == LANES and gr == LANES
    rw = 3 * d + dr + ir + gr
    gw = 2 * d
    n_grp = router_group_w.shape[-1]
    n_exp = router_expert_w.shape[-1]
    row = lambda a: a.astype(F32).reshape(1, -1)

    head_of_lane = jnp.arange(d, dtype=jnp.int32) // hd
    hsum = (head_of_lane[:, None] == jnp.arange(LANES, dtype=jnp.int32)[None, :]).astype(BF16)
    rt_w = jnp.zeros((d, LANES), F32).at[:, :n_exp].set(router_expert_w[l]).at[:, n_exp:n_exp + n_grp].set(
        router_group_w[l])
    rt_hi = rt_w.astype(BF16)
    rt_lo = (rt_w - rt_hi.astype(F32)).astype(BF16)
    rt_b = jnp.zeros((1, LANES), F32).at[0, :n_exp].set(router_expert_b[l]).at[0, n_exp:n_exp + n_grp].set(
        router_group_b[l])
    w_in_b = w_in[l].astype(BF16)
    w = dict(
        cw=cw, rw=rw, gw=gw, n_exp=n_exp, n_grp=n_grp,
        n1=row(norm1_g[l]), w_in_b=w_in_b, w_rw_b=w_in_b[:, 2 * cw:2 * cw + rw],
        dw=conv_dw_w[l].astype(F32), dwb=row(conv_dw_b[l]), lng=row(conv_ln_g[l]), lnb=row(conv_ln_b[l]),
        conv_out_b16=conv_out_w[l].astype(BF16), conv_out_bias=row(conv_out_b[l]),
        mu=row(rwkv_mu[l]), w0=row(rwkv_w0[l]), a0=row(rwkv_a0[l]),
        w2p=jnp.concatenate([rwkv_w2[l], jnp.zeros((ir, d), F32)], axis=0).astype(BF16),
        a2p=jnp.concatenate([jnp.zeros((dr, d), F32), rwkv_a2[l]], axis=0).astype(BF16),
        g2b=rwkv_g2[l].astype(BF16), k_k=row(rwkv_k_k[l]), k_a=row(rwkv_k_a[l]),
        hsum=hsum, hexp=hsum.T, r_k=row(rwkv_r_k[l]), gn_g=row(rwkv_gn_g[l]), gn_b=row(rwkv_gn_b[l]),
        rwkv_out_b16=rwkv_out_w[l].astype(BF16), w_o_b16=w_o[l].astype(BF16), n2=row(norm2_g[l]),
        rt_hi=rt_hi, rt_lo=rt_lo, rt_b=rt_b,
        wg_b=expert_w_gate[l], wu_b=expert_w_up[l], wd_b=expert_w_down[l],
    )
    fg = row(final_norm_g)
    bp = x_prompt.shape[0]
    taps = conv_dw_w.shape[1]
    yp, cbp, srp, sp = _layer(x_prompt.astype(F32), jnp.zeros((bp, taps - 1, cw), F32), jnp.zeros((bp, d), F32),
                              jnp.zeros((bp, heads, hd, hd), F32), w, fg)
    ys, cbs, srs, ss = _layer(x_sample.astype(F32), state_conv[l], state_shift[l], state_wkv[l], w, fg)
    sdt = state_wkv.dtype
    return (yp.astype(x_prompt.dtype), ys.astype(x_sample.dtype),
            cbp[None].astype(state_conv.dtype), srp[None].astype(state_shift.dtype), sp[None].astype(sdt),
            cbs[None].astype(state_conv.dtype), srs[None].astype(state_shift.dtype), ss[None].astype(sdt))
```
